```python
import jax, jax.numpy as jnp
from jax import lax
import numpy as np

D_MODEL = 1024
BATCH = 2
SEQ = 16384
DEPTH = 2

GRID_W = 64
CTX_LEN = 256
NA_HEADS = 8
HEAD_DIM = 64
NA_WIDTH = NA_HEADS * HEAD_DIM
WIN_H = 8
WIN_W = 16
SG_GROUPS = 4
SG_CHUNK = 128
SG_WIDTH = 512
D_FF = 2816
ROPE_THETA = 10000.0
EPS = 1e-6
N_MOD = 9
Q0, K0, V0 = 0, NA_WIDTH, 2 * NA_WIDTH
U0 = 3 * NA_WIDTH
VS0 = U0 + SG_WIDTH
G0 = VS0 + SG_WIDTH
IN_COLS = G0 + 2 * D_MODEL

kernel_name = "hybrid_na_gmlp_macaron_dit"


def rmsnorm(t, g):
    tf = t.astype(jnp.float32)
    y = tf * lax.rsqrt(jnp.mean(tf * tf, axis=-1, keepdims=True) + EPS)
    return (y * g.astype(jnp.float32)).astype(t.dtype)


def layernorm(t, g, b):
    tf = t.astype(jnp.float32)
    mu = jnp.mean(tf, axis=-1, keepdims=True)
    var = jnp.mean(jnp.square(tf - mu), axis=-1, keepdims=True)
    y = (tf - mu) * lax.rsqrt(var + EPS)
    return (y * g.astype(jnp.float32) + b.astype(jnp.float32)).astype(t.dtype)


def modulate(t, g, shift, scale):
    return rmsnorm(t, g) * (1.0 + scale) + shift


def ada_mod(cond, w, b):
    m = jax.nn.silu(cond) @ w + b
    return jnp.moveaxis(m.reshape(cond.shape[0], N_MOD, D_MODEL), 1, 0)[:, :, None, :]


def swiglu(h, w_up, w_down):
    a, b = jnp.split(h @ w_up, 2, axis=-1)
    return (jax.nn.silu(a) * b) @ w_down


def ffn_sublayer(t, mod, i, g, w_up, w_down):
    h = modulate(t, g, mod[3 * i], mod[3 * i + 1])
    return t + 0.5 * mod[3 * i + 2] * swiglu(h, w_up, w_down)


def heads(t):
    return t.reshape(*t.shape[:-1], NA_HEADS, HEAD_DIM)


def axial_rope(t, rows, cols):
    n_freq = HEAD_DIM // 4
    freqs = ROPE_THETA ** (-jnp.arange(n_freq, dtype=jnp.float32) / n_freq)
    ang = jnp.concatenate([rows[:, None] * freqs, cols[:, None] * freqs], axis=-1)
    cos = jnp.cos(ang)[None, :, None, :]
    sin = jnp.sin(ang)[None, :, None, :]
    tf = t.astype(jnp.float32).reshape(*t.shape[:-1], HEAD_DIM // 2, 2)
    e, o = tf[..., 0], tf[..., 1]
    out = jnp.stack([e * cos - o * sin, e * sin + o * cos], axis=-1).reshape(t.shape)
    return out.astype(t.dtype)


def neighbourhood_attention(q, k, v, k_ctx, v_ctx, rpb):
    B, N, H, hd = q.shape
    rows = N // GRID_W
    win_h = min(WIN_H, rows)
    n_nb = win_h * WIN_W
    scale = hd ** -0.5
    qg = q.reshape(B, rows, GRID_W, H, hd)
    kg = k.reshape(B, rows, GRID_W, H, hd)
    vg = v.reshape(B, rows, GRID_W, H, hd)
    cols = jnp.arange(GRID_W)
    col_start = jnp.clip(cols - WIN_W // 2, 0, GRID_W - WIN_W)
    col_idx = col_start[:, None] + jnp.arange(WIN_W)[None, :]
    dc = col_idx - cols[:, None] + (WIN_W - 1)
    rpb_c = rpb[:, :, dc]

    def row_block(r):
        rs = jnp.clip(r - WIN_H // 2, 0, rows - win_h)
        q_r = lax.dynamic_index_in_dim(qg, r, axis=1, keepdims=False)
        k_rows = lax.dynamic_slice_in_dim(kg, rs, win_h, axis=1)
        v_rows = lax.dynamic_slice_in_dim(vg, rs, win_h, axis=1)
        k_nb = jnp.moveaxis(k_rows[:, :, col_idx], 2, 1).reshape(B, GRID_W, n_nb, H, hd)
        v_nb = jnp.moveaxis(v_rows[:, :, col_idx], 2, 1).reshape(B, GRID_W, n_nb, H, hd)
        dr = rs + jnp.arange(win_h) - r + (WIN_H - 1)
        bias = jnp.moveaxis(jnp.take(rpb_c, dr, axis=1), 2, 1).reshape(H, GRID_W, n_nb)
        s_nb = jnp.einsum('bqhd,bqkhd->bhqk', q_r, k_nb).astype(jnp.float32) * scale \
            + bias.astype(jnp.float32)
        s_ctx = jnp.einsum('bqhd,bkhd->bhqk', q_r, k_ctx).astype(jnp.float32) * scale
        p = jax.nn.softmax(jnp.concatenate([s_nb, s_ctx], axis=-1), axis=-1).astype(v.dtype)
        return (jnp.einsum('bhqk,bqkhd->bqhd', p[..., :n_nb], v_nb)
                + jnp.einsum('bhqk,bkhd->bqhd', p[..., n_nb:], v_ctx))

    out = lax.map(row_block, jnp.arange(rows))
    return jnp.moveaxis(out, 0, 1).reshape(B, N, H * hd)


def context_attention(q, k, v):
    B, L, H, hd = q.shape
    s = jnp.einsum('bqhd,bkhd->bhqk', q, k).astype(jnp.float32) * (hd ** -0.5)
    p = jax.nn.softmax(s, axis=-1).astype(v.dtype)
    return jnp.einsum('bhqk,bkhd->bqhd', p, v).reshape(B, L, H * hd)


def chunk_spatial_gating(u, v, ln_g, ln_b, w_s, b_s):
    B, N, _ = v.shape
    vn = layernorm(v, ln_g, ln_b).reshape(B, N // SG_CHUNK, SG_CHUNK, SG_GROUPS, SG_WIDTH // SG_GROUPS)
    s = jnp.einsum('gpq,bnqgc->bnpgc', w_s, vn) + b_s.T[None, None, :, :, None]
    return u * s.reshape(B, N, SG_WIDTH)


def merge_branches(o_a, o_b, g_logits, b_gate, w_pa, w_pb, w_o):
    g_a, g_b = jnp.split(jax.nn.sigmoid(g_logits + b_gate), 2, axis=-1)
    return (g_a * (o_a @ w_pa) + g_b * (o_b @ w_pb)) @ w_o


def setup_inputs(seed: int = 0) -> dict:
    key = jax.random.key(seed)
    ks = jax.random.split(key, 24)
    f32 = jnp.float32
    nrm = lambda k, shape, s: jax.random.normal(k, shape, f32) * s
    D, L = D_MODEL, DEPTH
    return {
        "x": nrm(ks[0], (BATCH, SEQ, D), 1.0),
        "c": nrm(ks[1], (BATCH, D), 1.0),
        "ctx": nrm(ks[2], (BATCH, CTX_LEN, D), 1.0),
        "c_ctx": nrm(ks[3], (D,), 1.0),
        "w_ada": nrm(ks[4], (L, D, N_MOD * D), 0.5 * D ** -0.5),
        "b_ada": nrm(ks[5], (L, N_MOD * D), 0.02),
        "norm_g": 1.0 + nrm(ks[6], (L, 3, D), 0.02),
        "w_ff1_up": nrm(ks[7], (L, D, 2 * D_FF), D ** -0.5),
        "w_ff1_down": nrm(ks[8], (L, D_FF, D), D_FF ** -0.5),
        "w_in": nrm(ks[9], (L, D, IN_COLS), D ** -0.5),
        "b_gate": nrm(ks[10], (L, 2 * D), 0.02),
        "rpb": nrm(ks[11], (L, NA_HEADS, 2 * WIN_H - 1, 2 * WIN_W - 1), 0.1),
        "ln_v_g": 1.0 + nrm(ks[12], (L, SG_WIDTH), 0.02),
        "ln_v_b": nrm(ks[13], (L, SG_WIDTH), 0.02),
        "w_s": nrm(ks[14], (L, SG_GROUPS, SG_CHUNK, SG_CHUNK), 0.5 * SG_CHUNK ** -0.5),
        "b_s": 1.0 + nrm(ks[15], (L, SG_GROUPS, SG_CHUNK), 0.02),
        "w_pa": nrm(ks[16], (L, NA_WIDTH, D), NA_WIDTH ** -0.5),
        "w_pb": nrm(ks[17], (L, SG_WIDTH, D), SG_WIDTH ** -0.5),
        "w_o": nrm(ks[18], (L, D, D), D ** -0.5),
        "w_ff2_up": nrm(ks[19], (L, D, 2 * D_FF), D ** -0.5),
        "w_ff2_down": nrm(ks[20], (L, D_FF, D), D_FF ** -0.5),
        "final_g": 1.0 + nrm(ks[21], (D,), 0.02),
    }


def reference(x, c, ctx, c_ctx, w_ada, b_ada, norm_g, w_ff1_up, w_ff1_down, w_in, b_gate,
              rpb, ln_v_g, ln_v_b, w_s, b_s, w_pa, w_pb, w_o, w_ff2_up, w_ff2_down, final_g):
    N = x.shape[1]
    t = jnp.arange(N)
    pos_r = (t // GRID_W).astype(jnp.float32)
    pos_c = (t % GRID_W).astype(jnp.float32)
    gelu = jax.nn.gelu

    for l in range(DEPTH):
        last = l == DEPTH - 1
        mx = ada_mod(c, w_ada[l], b_ada[l])
        mc = ada_mod(c_ctx[None, :], w_ada[l], b_ada[l])

        x = ffn_sublayer(x, mx, 0, norm_g[l, 0], w_ff1_up[l], w_ff1_down[l])
        ctx = ffn_sublayer(ctx, mc, 0, norm_g[l, 0], w_ff1_up[l], w_ff1_down[l])

        hx = modulate(x, norm_g[l, 1], mx[3], mx[4])
        hc = modulate(ctx, norm_g[l, 1], mc[3], mc[4])
        px = hx @ w_in[l]
        if last:
            pc = hc @ w_in[l][:, K0:U0]
            kc, vc = heads(pc[..., :NA_WIDTH]), heads(pc[..., NA_WIDTH:])
        else:
            pc = hc @ w_in[l]
            kc, vc = heads(pc[..., K0:V0]), heads(pc[..., V0:U0])

        qx = axial_rope(heads(px[..., Q0:K0]), pos_r, pos_c)
        kx = axial_rope(heads(px[..., K0:V0]), pos_r, pos_c)
        o_a = neighbourhood_attention(qx, kx, heads(px[..., V0:U0]), kc, vc, rpb[l])
        o_b = chunk_spatial_gating(gelu(px[..., U0:VS0]), gelu(px[..., VS0:G0]),
                                   ln_v_g[l], ln_v_b[l], w_s[l], b_s[l])
        x = x + mx[5] * merge_branches(o_a, o_b, px[..., G0:], b_gate[l], w_pa[l], w_pb[l], w_o[l])

        if not last:
            o_ac = context_attention(heads(pc[..., Q0:K0]), kc, vc)
            o_bc = chunk_spatial_gating(gelu(pc[..., U0:VS0]), gelu(pc[..., VS0:G0]),
                                        ln_v_g[l], ln_v_b[l], w_s[l], b_s[l])
            ctx = ctx + mc[5] * merge_branches(o_ac, o_bc, pc[..., G0:], b_gate[l],
                                               w_pa[l], w_pb[l], w_o[l])
            ctx = ffn_sublayer(ctx, mc, 2, norm_g[l, 2], w_ff2_up[l], w_ff2_down[l])

        x = ffn_sublayer(x, mx, 2, norm_g[l, 2], w_ff2_up[l], w_ff2_down[l])

    return rmsnorm(x, final_g)
```

```python
import functools

import jax
import jax.numpy as jnp
from jax import lax
from jax.experimental import pallas as pl
from jax.experimental.pallas import tpu as pltpu

D_MODEL = 1024
GRID_W = 64
NA_HEADS = 8
HEAD_DIM = 64
NA_WIDTH = NA_HEADS * HEAD_DIM
WIN_H = 8
WIN_W = 16
SG_GROUPS = 4
SG_CHUNK = 128
SG_WIDTH = 512
D_FF = 2816
ROPE_THETA = 10000.0
EPS = 1e-6
N_MOD = 9
G0 = 3 * NA_WIDTH + 2 * SG_WIDTH
IN_COLS = G0 + 2 * D_MODEL

HEAD_GROUPS = 2
GROUP_HEADS = NA_HEADS // HEAD_GROUPS
GROUP_LANES = GROUP_HEADS * HEAD_DIM
ROPE_LANES = GROUP_LANES // 2
FF_CHUNK = 256
N_FF_CHUNKS = D_FF // FF_CHUNK
ADA_COLS = 1024
COND_ROWS = 8
NEG_BIAS = -1e30
VMEM_LIMIT = 56 * 1024 * 1024

BF16 = jnp.bfloat16
F32 = jnp.float32


def _dot(a, b):
    return jnp.dot(a, b, preferred_element_type=F32)


def _dot_nt(a, b):
    return lax.dot_general(a, b, (((1,), (1,)), ((), ())), preferred_element_type=F32)


def _rms_modulate(x, g, shift, scale):
    ms = jnp.mean(x * x, axis=-1, keepdims=True)
    return (x * lax.rsqrt(ms + EPS) * g) * (1.0 + scale) + shift


def _const_spec(shape):
    zeros = (0,) * len(shape)
    return pl.BlockSpec(shape, lambda *_: zeros, pipeline_mode=pl.Buffered(1))


def _ada_kernel(cond_ref, w_ref, b_ref, out_ref):
    c = cond_ref[...]
    s = c * jax.nn.sigmoid(c)
    w = w_ref[0]
    s_hi = s.astype(BF16)
    s_lo = (s - s_hi.astype(F32)).astype(BF16)
    w_hi = w.astype(BF16)
    w_lo = (w - w_hi.astype(F32)).astype(BF16)
    acc = _dot(s_hi, w_hi) + _dot(s_hi, w_lo) + _dot(s_lo, w_hi)
    out_ref[0] = acc + b_ref[0]


def _ada(cond, w_ada, b_ada):
    depth = w_ada.shape[0]
    cols = w_ada.shape[2]
    return pl.pallas_call(
        _ada_kernel,
        grid=(depth, cols // ADA_COLS),
        in_specs=[
            pl.BlockSpec((COND_ROWS, D_MODEL), lambda l, j: (0, 0)),
            pl.BlockSpec((1, D_MODEL, ADA_COLS), lambda l, j: (l, 0, j)),
            pl.BlockSpec((1, 1, ADA_COLS), lambda l, j: (l, 0, j)),
        ],
        out_specs=pl.BlockSpec((1, COND_ROWS, ADA_COLS), lambda l, j: (l, 0, j)),
        out_shape=jax.ShapeDtypeStruct((depth, COND_ROWS, cols), F32),
        compiler_params=pltpu.CompilerParams(
            dimension_semantics=("parallel", "parallel"), vmem_limit_bytes=VMEM_LIMIT),
        name="ada",
    )(cond, w_ada, b_ada.reshape(depth, 1, cols))


def _ffn_kernel(x_ref, mod_ref, g_ref, wa_ref, wb_ref, wd_ref, fg_ref, out_ref,
                h_scr, acc_scr, *, mod_row, final):
    x = x_ref[...]
    shift = mod_ref[mod_row:mod_row + 1, :]
    scale = mod_ref[mod_row + 1:mod_row + 2, :]
    gate = mod_ref[mod_row + 2:mod_row + 3, :]
    h_scr[...] = _rms_modulate(x, g_ref[...], shift, scale).astype(BF16)
    acc_scr[...] = jnp.zeros_like(acc_scr)

    def chunk(c, carry):
        h = h_scr[...]
        a = _dot(h, wa_ref[c])
        b = _dot(h, wb_ref[c])
        act = (a * jax.nn.sigmoid(a) * b).astype(BF16)
        acc_scr[...] += _dot(act, wd_ref[c])
        return carry

    lax.fori_loop(0, N_FF_CHUNKS, chunk, 0)
    o = x + (0.5 * gate) * acc_scr[...]
    if final:
        ms = jnp.mean(o * o, axis=-1, keepdims=True)
        o = o * lax.rsqrt(ms + EPS) * fg_ref[...]
    out_ref[...] = o


def _ffn(x, mod, mod_row, g, wa, wb, wd, final_g, *, final, tm):
    batch, n, _ = x.shape
    kern = functools.partial(_ffn_kernel, mod_row=mod_row, final=final)
    return pl.pallas_call(
        kern,
        grid=(batch, n // tm),
        in_specs=[
            pl.BlockSpec((None, tm, D_MODEL), lambda b, i: (b, i, 0)),
            pl.BlockSpec((None, N_MOD, D_MODEL), lambda b, i: (b, 0, 0)),
            _const_spec((1, D_MODEL)),
            _const_spec((N_FF_CHUNKS, D_MODEL, FF_CHUNK)),
            _const_spec((N_FF_CHUNKS, D_MODEL, FF_CHUNK)),
            _const_spec((N_FF_CHUNKS, FF_CHUNK, D_MODEL)),
            _const_spec((1, D_MODEL)),
        ],
        out_specs=pl.BlockSpec((None, tm, D_MODEL), lambda b, i: (b, i, 0)),
        out_shape=jax.ShapeDtypeStruct(x.shape, F32),
        scratch_shapes=[pltpu.VMEM((tm, D_MODEL), BF16), pltpu.VMEM((tm, D_MODEL), F32)],
        compiler_params=pltpu.CompilerParams(
            dimension_semantics=("parallel", "parallel"), vmem_limit_bytes=VMEM_LIMIT),
        name="ffn_final" if final else "ffn",
    )(x, mod, g.reshape(1, D_MODEL), wa, wb, wd, final_g.reshape(1, D_MODEL))


def _inproj_kernel(x_ref, mod_ref, g_ref, w_ref, cos_ref, sin_ref, lng_ref, lnb_ref,
                   q_ref, k_ref, v_ref, u_ref, vn_ref, gl_ref):
    x = x_ref[...]
    h = _rms_modulate(x, g_ref[...], mod_ref[3:4, :], mod_ref[4:5, :]).astype(BF16)
    cos = cos_ref[...]
    sin = sin_ref[...]

    def rope_store(t, out_ref, mult):
        for g in range(HEAD_GROUPS):
            lo = g * GROUP_LANES
            e = t[:, lo:lo + ROPE_LANES]
            o = t[:, lo + ROPE_LANES:lo + GROUP_LANES]
            out_ref[:, lo:lo + ROPE_LANES] = ((e * cos - o * sin) * mult).astype(BF16)
            out_ref[:, lo + ROPE_LANES:lo + GROUP_LANES] = ((e * sin + o * cos) * mult).astype(BF16)

    w = NA_WIDTH
    rope_store(_dot(h, w_ref[:, 0:w]), q_ref, HEAD_DIM ** -0.5)
    rope_store(_dot(h, w_ref[:, w:2 * w]), k_ref, 1.0)
    v_ref[...] = _dot(h, w_ref[:, 2 * w:3 * w]).astype(BF16)
    u0 = 3 * w
    u_ref[...] = jax.nn.gelu(_dot(h, w_ref[:, u0:u0 + SG_WIDTH])).astype(BF16)
    vs = jax.nn.gelu(_dot(h, w_ref[:, u0 + SG_WIDTH:G0]))
    mu = jnp.mean(vs, axis=-1, keepdims=True)
    dv = vs - mu
    var = jnp.mean(dv * dv, axis=-1, keepdims=True)
    vn_ref[...] = (dv * lax.rsqrt(var + EPS) * lng_ref[...] + lnb_ref[...]).astype(BF16)
    gl_ref[...] = _dot(h, w_ref[:, G0:IN_COLS])


def _inproj(x, mod, g, w_in, cos, sin, ln_g, ln_b, *, tm):
    batch, n, _ = x.shape
    tok = lambda width, dt: jax.ShapeDtypeStruct((batch, n, width), dt)
    tok_spec = lambda width: pl.BlockSpec((None, tm, width), lambda b, i: (b, i, 0))
    return pl.pallas_call(
        _inproj_kernel,
        grid=(batch, n // tm),
        in_specs=[
            tok_spec(D_MODEL),
            pl.BlockSpec((None, N_MOD, D_MODEL), lambda b, i: (b, 0, 0)),
            _const_spec((1, D_MODEL)),
            _const_spec((D_MODEL, IN_COLS)),
            pl.BlockSpec((tm, ROPE_LANES), lambda b, i: (i, 0)),
            pl.BlockSpec((tm, ROPE_LANES), lambda b, i: (i, 0)),
            _const_spec((1, SG_WIDTH)),
            _const_spec((1, SG_WIDTH)),
        ],
        out_specs=[tok_spec(NA_WIDTH), tok_spec(NA_WIDTH), tok_spec(NA_WIDTH),
                   tok_spec(SG_WIDTH), tok_spec(SG_WIDTH), tok_spec(2 * D_MODEL)],
        out_shape=[tok(NA_WIDTH, BF16), tok(NA_WIDTH, BF16), tok(NA_WIDTH, BF16),
                   tok(SG_WIDTH, BF16), tok(SG_WIDTH, BF16), tok(2 * D_MODEL, F32)],
        compiler_params=pltpu.CompilerParams(
            dimension_semantics=("parallel", "parallel"), vmem_limit_bytes=VMEM_LIMIT),
        name="inproj",
    )(x, mod, g.reshape(1, D_MODEL), w_in, cos, sin,
      ln_g.reshape(1, SG_WIDTH), ln_b.reshape(1, SG_WIDTH))


def _head_row_mask():
    row_head = lax.broadcasted_iota(jnp.int32, (GROUP_LANES, GROUP_LANES), 0) // GRID_W
    lane_head = (lax.broadcasted_iota(jnp.int32, (GROUP_LANES, GROUP_LANES), 1) % ROPE_LANES) // (HEAD_DIM // 2)
    return row_head == lane_head


def _unstack_heads(o, rows):
    lane_head = lax.broadcasted_iota(jnp.int32, (rows, GROUP_LANES), 1) // HEAD_DIM
    out = jnp.zeros((rows, GROUP_LANES), F32)
    for j in range(GROUP_HEADS):
        out = out + jnp.where(lane_head == j, o[j * rows:(j + 1) * rows, :], 0.0)
    return out


def _attn_kernel(q_ref, k_ref, v_ref, kc_ref, vc_ref, bias_ref, o_ref, *, rows, block_rows, halo_rows):
    r0 = pl.program_id(1) * block_rows
    halo_start = jnp.clip(r0 - WIN_H // 2, 0, rows - halo_rows)
    qmask = _head_row_mask()

    def row_body(ri, carry):
        r = r0 + ri
        rs = jnp.clip(r - WIN_H // 2, 0, rows - WIN_H)
        koff = pl.multiple_of((rs - halo_start) * GRID_W, GRID_W)
        d0 = rs - r + (WIN_H - 1)
        qoff = pl.multiple_of(ri * GRID_W, GRID_W)
        for g in range(HEAD_GROUPS):
            lanes = slice(g * GROUP_LANES, (g + 1) * GROUP_LANES)
            qg = q_ref[pl.ds(qoff, GRID_W), lanes]
            qs = jnp.where(qmask, jnp.concatenate([qg] * GROUP_HEADS, axis=0), jnp.zeros((), BF16))
            kw = k_ref[pl.ds(koff, WIN_H * GRID_W), lanes]
            s_nb = _dot_nt(qs, kw) + bias_ref[d0, g]
            s_c = _dot_nt(qs, kc_ref[:, lanes])
            m = jnp.maximum(jnp.max(s_nb, axis=-1, keepdims=True),
                            jnp.max(s_c, axis=-1, keepdims=True))
            p_nb = jnp.exp(s_nb - m)
            p_c = jnp.exp(s_c - m)
            denom = jnp.sum(p_nb, axis=-1, keepdims=True) + jnp.sum(p_c, axis=-1, keepdims=True)
            vw = v_ref[pl.ds(koff, WIN_H * GRID_W), lanes]
            o = _dot(p_nb.astype(BF16), vw) + _dot(p_c.astype(BF16), vc_ref[:, lanes])
            o = o / denom
            o_ref[pl.ds(qoff, GRID_W), lanes] = _unstack_heads(o, GRID_W).astype(BF16)
        return carry

    lax.fori_loop(0, block_rows, row_body, 0)


def _attn(q, k, v, kc, vc, bias):
    batch, n, _ = q.shape
    ctx_len = kc.shape[1]
    rows = n // GRID_W
    block_rows = 8
    halo_rows = block_rows + WIN_H
    tm = block_rows * GRID_W

    def halo_map(b, i):
        start = jnp.clip(i * block_rows - WIN_H // 2, 0, rows - halo_rows)
        return (b, start * GRID_W, 0)

    halo_spec = pl.BlockSpec((None, pl.Element(halo_rows * GRID_W), pl.Element(NA_WIDTH)), halo_map)
    kern = functools.partial(_attn_kernel, rows=rows, block_rows=block_rows, halo_rows=halo_rows)
    return pl.pallas_call(
        kern,
        grid=(batch, rows // block_rows),
        in_specs=[
            pl.BlockSpec((None, tm, NA_WIDTH), lambda b, i: (b, i, 0)),
            halo_spec,
            halo_spec,
            pl.BlockSpec((None, ctx_len, NA_WIDTH), lambda b, i: (b, 0, 0)),
            pl.BlockSpec((None, ctx_len, NA_WIDTH), lambda b, i: (b, 0, 0)),
            _const_spec(bias.shape),
        ],
        out_specs=pl.BlockSpec((None, tm, NA_WIDTH), lambda b, i: (b, i, 0)),
        out_shape=jax.ShapeDtypeStruct((batch, n, NA_WIDTH), BF16),
        compiler_params=pltpu.CompilerParams(
            dimension_semantics=("parallel", "parallel"), vmem_limit_bytes=VMEM_LIMIT),
        name="nbr_attn",
    )(q, k, v, kc, vc, bias)


def _ctx_attn_kernel(q_ref, k_ref, v_ref, o_ref):
    n = q_ref.shape[0]
    lane_head_qk = (lax.broadcasted_iota(jnp.int32, (n, GROUP_LANES), 1) % ROPE_LANES) // (HEAD_DIM // 2)
    lane_head_v = lax.broadcasted_iota(jnp.int32, (n, GROUP_LANES), 1) // HEAD_DIM
    for g in range(HEAD_GROUPS):
        lanes = slice(g * GROUP_LANES, (g + 1) * GROUP_LANES)
        qg = q_ref[:, lanes]
        kg = k_ref[:, lanes]
        vg = v_ref[:, lanes]
        out = jnp.zeros((n, GROUP_LANES), F32)
        for j in range(GROUP_HEADS):
            qj = jnp.where(lane_head_qk == j, qg, jnp.zeros((), BF16))
            s = _dot_nt(qj, kg)
            p = jnp.exp(s - jnp.max(s, axis=-1, keepdims=True))
            o = _dot(p.astype(BF16), vg) / jnp.sum(p, axis=-1, keepdims=True)
            out = out + jnp.where(lane_head_v == j, o, 0.0)
        o_ref[:, lanes] = out.astype(BF16)


def _ctx_attn(q, k, v):
    batch, n, width = q.shape
    spec = pl.BlockSpec((None, n, width), lambda b: (b, 0, 0))
    return pl.pallas_call(
        _ctx_attn_kernel,
        grid=(batch,),
        in_specs=[spec, spec, spec],
        out_specs=spec,
        out_shape=jax.ShapeDtypeStruct(q.shape, BF16),
        compiler_params=pltpu.CompilerParams(dimension_semantics=("parallel",)),
        name="ctx_attn",
    )(q, k, v)


def _merge_kernel(x_ref, oa_ref, u_ref, vn_ref, gl_ref, mod_ref, ws_ref, bs_ref, bg_ref,
                  wpa_ref, wpb_ref, wo_ref, out_ref, ob_scr):
    tm = x_ref.shape[0]
    gw = SG_WIDTH // SG_GROUPS
    for n in range(tm // SG_CHUNK):
        rows = slice(n * SG_CHUNK, (n + 1) * SG_CHUNK)
        for g in range(SG_GROUPS):
            lanes = slice(g * gw, (g + 1) * gw)
            s = _dot(ws_ref[g], vn_ref[rows, lanes]) + bs_ref[:, lanes]
            ob_scr[rows, lanes] = (u_ref[rows, lanes].astype(F32) * s).astype(BF16)
    a = _dot(oa_ref[...], wpa_ref[...])
    b = _dot(ob_scr[...], wpb_ref[...])
    gates = jax.nn.sigmoid(gl_ref[...] + bg_ref[...])
    mixed = (gates[:, :D_MODEL] * a + gates[:, D_MODEL:] * b).astype(BF16)
    out_ref[...] = x_ref[...] + mod_ref[5:6, :] * _dot(mixed, wo_ref[...])


def _merge(x, oa, u, vn, gl, mod, w_s, bs_full, b_gate, w_pa, w_pb, w_o, *, tm):
    batch, n, _ = x.shape
    tok_spec = lambda width: pl.BlockSpec((None, tm, width), lambda b, i: (b, i, 0))
    return pl.pallas_call(
        _merge_kernel,
        grid=(batch, n // tm),
        in_specs=[
            tok_spec(D_MODEL), tok_spec(NA_WIDTH), tok_spec(SG_WIDTH), tok_spec(SG_WIDTH),
            tok_spec(2 * D_MODEL),
            pl.BlockSpec((None, N_MOD, D_MODEL), lambda b, i: (b, 0, 0)),
            _const_spec(w_s.shape),
            _const_spec(bs_full.shape),
            _const_spec((1, 2 * D_MODEL)),
            _const_spec(w_pa.shape),
            _const_spec(w_pb.shape),
            _const_spec(w_o.shape),
        ],
        out_specs=tok_spec(D_MODEL),
        out_shape=jax.ShapeDtypeStruct(x.shape, F32),
        scratch_shapes=[pltpu.VMEM((tm, SG_WIDTH), BF16)],
        compiler_params=pltpu.CompilerParams(
            dimension_semantics=("parallel", "parallel"), vmem_limit_bytes=VMEM_LIMIT),
        name="merge",
    )(x, oa, u, vn, gl, mod, w_s, bs_full, b_gate.reshape(1, 2 * D_MODEL), w_pa, w_pb, w_o)


def _qk_column_order():
    n = jnp.arange(NA_WIDTH)
    g = n // GROUP_LANES
    half = (n % GROUP_LANES) // ROPE_LANES
    j = (n % ROPE_LANES) // (HEAD_DIM // 2)
    i = n % (HEAD_DIM // 2)
    return HEAD_DIM * (GROUP_HEADS * g + j) + 2 * i + half


def _rope_tables(n):
    t = jnp.arange(n)
    pos_r = (t // GRID_W).astype(F32)
    pos_c = (t % GRID_W).astype(F32)
    n_freq = HEAD_DIM // 4
    freqs = ROPE_THETA ** (-jnp.arange(n_freq, dtype=F32) / n_freq)
    ang = jnp.concatenate([pos_r[:, None] * freqs, pos_c[:, None] * freqs], axis=-1)
    return jnp.tile(jnp.cos(ang), (1, GROUP_HEADS)), jnp.tile(jnp.sin(ang), (1, GROUP_HEADS))


def _bias_tables(rpb):
    qc = jnp.arange(GRID_W)
    kc = jnp.arange(GRID_W)
    cs = jnp.clip(qc - WIN_W // 2, 0, GRID_W - WIN_W)
    inband = (kc[None, :] >= cs[:, None]) & (kc[None, :] < cs[:, None] + WIN_W)
    dc = jnp.clip(kc[None, :] - qc[:, None] + (WIN_W - 1), 0, 2 * WIN_W - 2)
    dr = jnp.arange(WIN_H)[:, None] + jnp.arange(WIN_H)[None, :]
    t = rpb[:, dr][:, :, :, dc]
    t = jnp.where(inband, t, NEG_BIAS)
    t = t.reshape(HEAD_GROUPS, GROUP_HEADS, WIN_H, WIN_H, GRID_W, GRID_W)
    t = t.transpose(2, 0, 1, 4, 3, 5)
    return t.reshape(WIN_H, HEAD_GROUPS, GROUP_HEADS * GRID_W, WIN_H * GRID_W).astype(F32)


def _ff_weights(w_up, w_down):
    wa = w_up[:, :D_FF].reshape(D_MODEL, N_FF_CHUNKS, FF_CHUNK).transpose(1, 0, 2).astype(BF16)
    wb = w_up[:, D_FF:].reshape(D_MODEL, N_FF_CHUNKS, FF_CHUNK).transpose(1, 0, 2).astype(BF16)
    wd = w_down.reshape(N_FF_CHUNKS, FF_CHUNK, D_MODEL).astype(BF16)
    return wa, wb, wd


def kernel(x, c, ctx, c_ctx, w_ada, b_ada, norm_g, w_ff1_up, w_ff1_down, w_in, b_gate, rpb,
           ln_v_g, ln_v_b, w_s, b_s, w_pa, w_pb, w_o, w_ff2_up, w_ff2_down, final_g):
    batch, n, _ = x.shape
    ctx_len = ctx.shape[1]
    depth = w_ada.shape[0]
    assert n % (8 * GRID_W) == 0 and ctx_len % SG_CHUNK == 0 and batch + 1 <= COND_ROWS
    tm_x, tm_c = 512, ctx_len

    cond = jnp.zeros((COND_ROWS, D_MODEL), F32).at[:batch].set(c).at[batch].set(c_ctx)
    mods = _ada(cond, w_ada, b_ada).reshape(depth, COND_ROWS, N_MOD, D_MODEL)

    cos_x, sin_x = _rope_tables(n)
    cos_c = jnp.ones((ctx_len, ROPE_LANES), F32)
    sin_c = jnp.zeros((ctx_len, ROPE_LANES), F32)
    order = _qk_column_order()

    for l in range(depth):
        last = l == depth - 1
        mx = mods[l, :batch]
        mc = jnp.broadcast_to(mods[l, batch:batch + 1], (batch, N_MOD, D_MODEL))
        ff1 = _ff_weights(w_ff1_up[l], w_ff1_down[l])
        ff2 = _ff_weights(w_ff2_up[l], w_ff2_down[l])
        w_in_l = jnp.concatenate(
            [w_in[l][:, order], w_in[l][:, NA_WIDTH + order], w_in[l][:, 2 * NA_WIDTH:]], axis=1).astype(BF16)
        bias = _bias_tables(rpb[l])
        bs_full = jnp.repeat(b_s[l].T, SG_WIDTH // SG_GROUPS, axis=1)
        w_s_l = w_s[l].astype(BF16)
        w_pa_l, w_pb_l, w_o_l = w_pa[l].astype(BF16), w_pb[l].astype(BF16), w_o[l].astype(BF16)

        x = _ffn(x, mx, 0, norm_g[l, 0], *ff1, final_g, final=False, tm=tm_x)
        ctx = _ffn(ctx, mc, 0, norm_g[l, 0], *ff1, final_g, final=False, tm=tm_c)

        qx, kx, vx, ux, vnx, glx = _inproj(x, mx, norm_g[l, 1], w_in_l, cos_x, sin_x,
                                           ln_v_g[l], ln_v_b[l], tm=tm_x)
        qc, kc, vc, uc, vnc, glc = _inproj(ctx, mc, norm_g[l, 1], w_in_l, cos_c, sin_c,
                                           ln_v_g[l], ln_v_b[l], tm=tm_c)
        oa = _attn(qx, kx, vx, kc, vc, bias)
        x = _merge(x, oa, ux, vnx, glx, mx, w_s_l, bs_full, b_gate[l], w_pa_l, w_pb_l, w_o_l, tm=tm_x)

        if not last:
            oac = _ctx_attn(qc, kc, vc)
            ctx = _merge(ctx, oac, uc, vnc, glc, mc, w_s_l, bs_full, b_gate[l], w_pa_l, w_pb_l, w_o_l,
                         tm=tm_c)
            ctx = _ffn(ctx, mc, 6, norm_g[l, 2], *ff2, final_g, final=False, tm=tm_c)

        x = _ffn(x, mx, 6, norm_g[l, 2], *ff2, final_g, final=last, tm=tm_x)

    return x
```

```python
import functools

import jax
import jax.numpy as jnp
from jax import lax
from jax.experimental import pallas as pl
from jax.experimental.pallas import tpu as pltpu

D_MODEL = 1024
GRID_W = 64
NA_HEADS = 8
HEAD_DIM = 64
NA_WIDTH = NA_HEADS * HEAD_DIM
WIN_H = 8
WIN_W = 16
SG_GROUPS = 4
SG_CHUNK = 128
SG_WIDTH = 512
D_FF = 2816
ROPE_THETA = 10000.0
EPS = 1e-6
N_MOD = 9
G0 = 3 * NA_WIDTH + 2 * SG_WIDTH
IN_COLS = G0 + 2 * D_MODEL

HEAD_GROUPS = 2
GROUP_HEADS = NA_HEADS // HEAD_GROUPS
GROUP_LANES = GROUP_HEADS * HEAD_DIM
ROPE_LANES = GROUP_LANES // 2
FF_CHUNK = 256
N_FF_CHUNKS = D_FF // FF_CHUNK
ADA_COLS = 1024
COND_ROWS = 8
NEG_BIAS = -1e30
VMEM_LIMIT = 56 * 1024 * 1024

BF16 = jnp.bfloat16
F32 = jnp.float32


def _dot(a, b):
    return jnp.dot(a, b, preferred_element_type=F32)


def _dot_nt(a, b):
    return lax.dot_general(a, b, (((1,), (1,)), ((), ())), preferred_element_type=F32)


def _rms_modulate(x, g, shift, scale):
    ms = jnp.mean(x * x, axis=-1, keepdims=True)
    return (x * lax.rsqrt(ms + EPS) * g) * (1.0 + scale) + shift


def _const_spec(shape):
    zeros = (0,) * len(shape)
    return pl.BlockSpec(shape, lambda *_: zeros, pipeline_mode=pl.Buffered(1))


def _ada_kernel(cond_ref, w_ref, b_ref, out_ref):
    c = cond_ref[...]
    s = c * jax.nn.sigmoid(c)
    w = w_ref[0]
    s_hi = s.astype(BF16)
    s_lo = (s - s_hi.astype(F32)).astype(BF16)
    w_hi = w.astype(BF16)
    w_lo = (w - w_hi.astype(F32)).astype(BF16)
    acc = _dot(s_hi, w_hi) + _dot(s_hi, w_lo) + _dot(s_lo, w_hi)
    out_ref[0] = acc + b_ref[0]


def _ada(cond, w_ada, b_ada):
    depth = w_ada.shape[0]
    cols = w_ada.shape[2]
    return pl.pallas_call(
        _ada_kernel,
        grid=(depth, cols // ADA_COLS),
        in_specs=[
            pl.BlockSpec((COND_ROWS, D_MODEL), lambda l, j: (0, 0)),
            pl.BlockSpec((1, D_MODEL, ADA_COLS), lambda l, j: (l, 0, j)),
            pl.BlockSpec((1, 1, ADA_COLS), lambda l, j: (l, 0, j)),
        ],
        out_specs=pl.BlockSpec((1, COND_ROWS, ADA_COLS), lambda l, j: (l, 0, j)),
        out_shape=jax.ShapeDtypeStruct((depth, COND_ROWS, cols), F32),
        compiler_params=pltpu.CompilerParams(
            dimension_semantics=("parallel", "parallel"), vmem_limit_bytes=VMEM_LIMIT),
        name="ada",
    )(cond, w_ada, b_ada.reshape(depth, 1, cols))


def _ffn_kernel(x_ref, mod_ref, g_ref, wup_ref, wd_ref, fg_ref, out_ref,
                h_scr, acc_scr, *, mod_row, final):
    x = x_ref[...]
    shift = mod_ref[mod_row:mod_row + 1, :]
    scale = mod_ref[mod_row + 1:mod_row + 2, :]
    gate = mod_ref[mod_row + 2:mod_row + 3, :]
    h_scr[...] = _rms_modulate(x, g_ref[...], shift, scale).astype(BF16)

    for c in range(N_FF_CHUNKS):
        cols = slice(c * FF_CHUNK, (c + 1) * FF_CHUNK)
        gate_cols = slice(D_FF + c * FF_CHUNK, D_FF + (c + 1) * FF_CHUNK)
        h = h_scr[...]
        a = _dot(h, wup_ref[:, cols])
        b = _dot(h, wup_ref[:, gate_cols])
        act = (a * jax.nn.sigmoid(a) * b).astype(BF16)
        y = _dot(act, wd_ref[cols, :])
        if c == 0:
            acc_scr[...] = y
        else:
            acc_scr[...] += y
    o = x + (0.5 * gate) * acc_scr[...]
    if final:
        ms = jnp.mean(o * o, axis=-1, keepdims=True)
        o = o * lax.rsqrt(ms + EPS) * fg_ref[...]
    out_ref[...] = o


def _ffn(x, mod, mod_row, g, w_up, w_down, final_g, *, final, tm):
    batch, n, _ = x.shape
    kern = functools.partial(_ffn_kernel, mod_row=mod_row, final=final)
    return pl.pallas_call(
        kern,
        grid=(batch, n // tm),
        in_specs=[
            pl.BlockSpec((None, tm, D_MODEL), lambda b, i: (b, i, 0)),
            pl.BlockSpec((None, N_MOD, D_MODEL), lambda b, i: (b, 0, 0)),
            _const_spec((1, D_MODEL)),
            _const_spec((D_MODEL, 2 * D_FF)),
            _const_spec((D_FF, D_MODEL)),
            _const_spec((1, D_MODEL)),
        ],
        out_specs=pl.BlockSpec((None, tm, D_MODEL), lambda b, i: (b, i, 0)),
        out_shape=jax.ShapeDtypeStruct(x.shape, F32),
        scratch_shapes=[pltpu.VMEM((tm, D_MODEL), BF16), pltpu.VMEM((tm, D_MODEL), F32)],
        compiler_params=pltpu.CompilerParams(
            dimension_semantics=("parallel", "parallel"), vmem_limit_bytes=VMEM_LIMIT),
        name="ffn_final" if final else "ffn",
    )(x, mod, g.reshape(1, D_MODEL), w_up, w_down, final_g.reshape(1, D_MODEL))


def _deinterleave_matrix():
    old = lax.broadcasted_iota(jnp.int32, (GROUP_LANES, GROUP_LANES), 0)
    new = lax.broadcasted_iota(jnp.int32, (GROUP_LANES, GROUP_LANES), 1)
    half = new // ROPE_LANES
    head = (new % ROPE_LANES) // (HEAD_DIM // 2)
    pair = new % (HEAD_DIM // 2)
    return jnp.where(old == HEAD_DIM * head + 2 * pair + half, 1.0, 0.0).astype(BF16)


def _inproj_kernel(x_ref, mod_ref, g_ref, w_ref, cos_ref, sin_ref, lng_ref, lnb_ref,
                   q_ref, k_ref, v_ref, u_ref, vn_ref, gl_ref, wqk_scr):
    @pl.when((pl.program_id(0) == 0) & (pl.program_id(1) == 0))
    def _():
        perm = _deinterleave_matrix()
        for t in range(2 * HEAD_GROUPS):
            cols = slice(t * GROUP_LANES, (t + 1) * GROUP_LANES)
            wqk_scr[:, cols] = _dot(w_ref[:, cols], perm).astype(BF16)

    x = x_ref[...]
    h = _rms_modulate(x, g_ref[...], mod_ref[3:4, :], mod_ref[4:5, :]).astype(BF16)
    cos = cos_ref[...]
    sin = sin_ref[...]

    def rope_store(t, out_ref, mult):
        for g in range(HEAD_GROUPS):
            lo = g * GROUP_LANES
            e = t[:, lo:lo + ROPE_LANES]
            o = t[:, lo + ROPE_LANES:lo + GROUP_LANES]
            out_ref[:, lo:lo + ROPE_LANES] = ((e * cos - o * sin) * mult).astype(BF16)
            out_ref[:, lo + ROPE_LANES:lo + GROUP_LANES] = ((e * sin + o * cos) * mult).astype(BF16)

    w = NA_WIDTH
    rope_store(_dot(h, wqk_scr[:, 0:w]), q_ref, HEAD_DIM ** -0.5)
    rope_store(_dot(h, wqk_scr[:, w:2 * w]), k_ref, 1.0)
    v_ref[...] = _dot(h, w_ref[:, 2 * w:3 * w]).astype(BF16)
    u0 = 3 * w
    u_ref[...] = jax.nn.gelu(_dot(h, w_ref[:, u0:u0 + SG_WIDTH])).astype(BF16)
    vs = jax.nn.gelu(_dot(h, w_ref[:, u0 + SG_WIDTH:G0]))
    mu = jnp.mean(vs, axis=-1, keepdims=True)
    dv = vs - mu
    var = jnp.mean(dv * dv, axis=-1, keepdims=True)
    vn_ref[...] = (dv * lax.rsqrt(var + EPS) * lng_ref[...] + lnb_ref[...]).astype(BF16)
    gl_ref[...] = _dot(h, w_ref[:, G0:IN_COLS])


def _inproj(x, mod, g, w_in, cos, sin, ln_g, ln_b, *, tm):
    batch, n, _ = x.shape
    tok = lambda width, dt: jax.ShapeDtypeStruct((batch, n, width), dt)
    tok_spec = lambda width: pl.BlockSpec((None, tm, width), lambda b, i: (b, i, 0))
    return pl.pallas_call(
        _inproj_kernel,
        grid=(batch, n // tm),
        in_specs=[
            tok_spec(D_MODEL),
            pl.BlockSpec((None, N_MOD, D_MODEL), lambda b, i: (b, 0, 0)),
            _const_spec((1, D_MODEL)),
            _const_spec((D_MODEL, IN_COLS)),
            pl.BlockSpec((tm, ROPE_LANES), lambda b, i: (i, 0)),
            pl.BlockSpec((tm, ROPE_LANES), lambda b, i: (i, 0)),
            _const_spec((1, SG_WIDTH)),
            _const_spec((1, SG_WIDTH)),
        ],
        out_specs=[tok_spec(NA_WIDTH), tok_spec(NA_WIDTH), tok_spec(NA_WIDTH),
                   tok_spec(SG_WIDTH), tok_spec(SG_WIDTH), tok_spec(2 * D_MODEL)],
        out_shape=[tok(NA_WIDTH, BF16), tok(NA_WIDTH, BF16), tok(NA_WIDTH, BF16),
                   tok(SG_WIDTH, BF16), tok(SG_WIDTH, BF16), tok(2 * D_MODEL, F32)],
        scratch_shapes=[pltpu.VMEM((D_MODEL, 2 * NA_WIDTH), BF16)],
        compiler_params=pltpu.CompilerParams(
            dimension_semantics=("arbitrary", "arbitrary"), vmem_limit_bytes=VMEM_LIMIT),
        name="inproj",
    )(x, mod, g.reshape(1, D_MODEL), w_in, cos, sin,
      ln_g.reshape(1, SG_WIDTH), ln_b.reshape(1, SG_WIDTH))


def _head_row_mask():
    row_head = lax.broadcasted_iota(jnp.int32, (GROUP_LANES, GROUP_LANES), 0) // GRID_W
    lane_head = (lax.broadcasted_iota(jnp.int32, (GROUP_LANES, GROUP_LANES), 1) % ROPE_LANES) // (HEAD_DIM // 2)
    return row_head == lane_head


def _unstack_heads(o, rows):
    lane_head = lax.broadcasted_iota(jnp.int32, (rows, GROUP_LANES), 1) // HEAD_DIM
    out = jnp.zeros((rows, GROUP_LANES), F32)
    for j in range(GROUP_HEADS):
        out = out + jnp.where(lane_head == j, o[j * rows:(j + 1) * rows, :], 0.0)
    return out


def _attn_kernel(q_ref, k_ref, v_ref, kc_ref, vc_ref, bias_ref, o_ref, *, rows, block_rows, halo_rows):
    r0 = pl.program_id(1) * block_rows
    halo_start = jnp.clip(r0 - WIN_H // 2, 0, rows - halo_rows)
    qmask = _head_row_mask()

    def row_body(ri, carry):
        r = r0 + ri
        rs = jnp.clip(r - WIN_H // 2, 0, rows - WIN_H)
        koff = pl.multiple_of((rs - halo_start) * GRID_W, GRID_W)
        d0 = rs - r + (WIN_H - 1)
        qoff = ri * GRID_W
        for g in range(HEAD_GROUPS):
            lanes = slice(g * GROUP_LANES, (g + 1) * GROUP_LANES)
            qg = q_ref[pl.ds(qoff, GRID_W), lanes]
            qs = jnp.where(qmask, jnp.concatenate([qg] * GROUP_HEADS, axis=0), jnp.zeros((), BF16))
            kw = k_ref[pl.ds(koff, WIN_H * GRID_W), lanes]
            s_nb = _dot_nt(qs, kw) + bias_ref[d0, g]
            s_c = _dot_nt(qs, kc_ref[:, lanes])
            m = jnp.maximum(jnp.max(s_nb, axis=-1, keepdims=True),
                            jnp.max(s_c, axis=-1, keepdims=True))
            p_nb = jnp.exp(s_nb - m)
            p_c = jnp.exp(s_c - m)
            denom = jnp.sum(p_nb, axis=-1, keepdims=True) + jnp.sum(p_c, axis=-1, keepdims=True)
            vw = v_ref[pl.ds(koff, WIN_H * GRID_W), lanes]
            o = _dot(p_nb.astype(BF16), vw) + _dot(p_c.astype(BF16), vc_ref[:, lanes])
            o = o / denom
            o_ref[pl.ds(qoff, GRID_W), lanes] = _unstack_heads(o, GRID_W).astype(BF16)
        return carry

    for ri in range(block_rows):
        row_body(ri, 0)


def _attn(q, k, v, kc, vc, bias):
    batch, n, _ = q.shape
    ctx_len = kc.shape[1]
    rows = n // GRID_W
    block_rows = 8
    halo_rows = block_rows + WIN_H
    tm = block_rows * GRID_W

    def halo_map(b, i):
        start = jnp.clip(i * block_rows - WIN_H // 2, 0, rows - halo_rows)
        return (b, start * GRID_W, 0)

    halo_spec = pl.BlockSpec((None, pl.Element(halo_rows * GRID_W), pl.Element(NA_WIDTH)), halo_map)
    kern = functools.partial(_attn_kernel, rows=rows, block_rows=block_rows, halo_rows=halo_rows)
    return pl.pallas_call(
        kern,
        grid=(batch, rows // block_rows),
        in_specs=[
            pl.BlockSpec((None, tm, NA_WIDTH), lambda b, i: (b, i, 0)),
            halo_spec,
            halo_spec,
            pl.BlockSpec((None, ctx_len, NA_WIDTH), lambda b, i: (b, 0, 0)),
            pl.BlockSpec((None, ctx_len, NA_WIDTH), lambda b, i: (b, 0, 0)),
            _const_spec(bias.shape),
        ],
        out_specs=pl.BlockSpec((None, tm, NA_WIDTH), lambda b, i: (b, i, 0)),
        out_shape=jax.ShapeDtypeStruct((batch, n, NA_WIDTH), BF16),
        compiler_params=pltpu.CompilerParams(
            dimension_semantics=("parallel", "parallel"), vmem_limit_bytes=VMEM_LIMIT),
        name="nbr_attn",
    )(q, k, v, kc, vc, bias)


def _ctx_attn_kernel(q_ref, k_ref, v_ref, o_ref):
    n = q_ref.shape[0]
    lane_head_qk = (lax.broadcasted_iota(jnp.int32, (n, GROUP_LANES), 1) % ROPE_LANES) // (HEAD_DIM // 2)
    lane_head_v = lax.broadcasted_iota(jnp.int32, (n, GROUP_LANES), 1) // HEAD_DIM
    for g in range(HEAD_GROUPS):
        lanes = slice(g * GROUP_LANES, (g + 1) * GROUP_LANES)
        qg = q_ref[:, lanes]
        kg = k_ref[:, lanes]
        vg = v_ref[:, lanes]
        out = jnp.zeros((n, GROUP_LANES), F32)
        for j in range(GROUP_HEADS):
            qj = jnp.where(lane_head_qk == j, qg, jnp.zeros((), BF16))
            s = _dot_nt(qj, kg)
            p = jnp.exp(s - jnp.max(s, axis=-1, keepdims=True))
            o = _dot(p.astype(BF16), vg) / jnp.sum(p, axis=-1, keepdims=True)
            out = out + jnp.where(lane_head_v == j, o, 0.0)
        o_ref[:, lanes] = out.astype(BF16)


def _ctx_attn(q, k, v):
    batch, n, width = q.shape
    spec = pl.BlockSpec((None, n, width), lambda b: (b, 0, 0))
    return pl.pallas_call(
        _ctx_attn_kernel,
        grid=(batch,),
        in_specs=[spec, spec, spec],
        out_specs=spec,
        out_shape=jax.ShapeDtypeStruct(q.shape, BF16),
        compiler_params=pltpu.CompilerParams(dimension_semantics=("parallel",)),
        name="ctx_attn",
    )(q, k, v)


def _merge_kernel(x_ref, oa_ref, u_ref, vn_ref, gl_ref, mod_ref, ws_ref, bs_ref, bg_ref,
                  wpa_ref, wpb_ref, wo_ref, out_ref, ob_scr):
    tm = x_ref.shape[0]
    gw = SG_WIDTH // SG_GROUPS
    for n in range(tm // SG_CHUNK):
        rows = slice(n * SG_CHUNK, (n + 1) * SG_CHUNK)
        for g in range(SG_GROUPS):
            lanes = slice(g * gw, (g + 1) * gw)
            s = _dot(ws_ref[g], vn_ref[rows, lanes]) + bs_ref[:, lanes]
            ob_scr[rows, lanes] = (u_ref[rows, lanes].astype(F32) * s).astype(BF16)
    a = _dot(oa_ref[...], wpa_ref[...])
    b = _dot(ob_scr[...], wpb_ref[...])
    gates = jax.nn.sigmoid(gl_ref[...] + bg_ref[...])
    mixed = (gates[:, :D_MODEL] * a + gates[:, D_MODEL:] * b).astype(BF16)
    out_ref[...] = x_ref[...] + mod_ref[5:6, :] * _dot(mixed, wo_ref[...])


def _merge(x, oa, u, vn, gl, mod, w_s, bs_full, b_gate, w_pa, w_pb, w_o, *, tm):
    batch, n, _ = x.shape
    tok_spec = lambda width: pl.BlockSpec((None, tm, width), lambda b, i: (b, i, 0))
    return pl.pallas_call(
        _merge_kernel,
        grid=(batch, n // tm),
        in_specs=[
            tok_spec(D_MODEL), tok_spec(NA_WIDTH), tok_spec(SG_WIDTH), tok_spec(SG_WIDTH),
            tok_spec(2 * D_MODEL),
            pl.BlockSpec((None, N_MOD, D_MODEL), lambda b, i: (b, 0, 0)),
            _const_spec(w_s.shape),
            _const_spec(bs_full.shape),
            _const_spec((1, 2 * D_MODEL)),
            _const_spec(w_pa.shape),
            _const_spec(w_pb.shape),
            _const_spec(w_o.shape),
        ],
        out_specs=tok_spec(D_MODEL),
        out_shape=jax.ShapeDtypeStruct(x.shape, F32),
        scratch_shapes=[pltpu.VMEM((tm, SG_WIDTH), BF16)],
        compiler_params=pltpu.CompilerParams(
            dimension_semantics=("parallel", "parallel"), vmem_limit_bytes=VMEM_LIMIT),
        name="merge",
    )(x, oa, u, vn, gl, mod, w_s, bs_full, b_gate.reshape(1, 2 * D_MODEL), w_pa, w_pb, w_o)


def _rope_tables(n):
    t = jnp.arange(n)
    pos_r = (t // GRID_W).astype(F32)
    pos_c = (t % GRID_W).astype(F32)
    n_freq = HEAD_DIM // 4
    freqs = ROPE_THETA ** (-jnp.arange(n_freq, dtype=F32) / n_freq)
    ang = jnp.concatenate([pos_r[:, None] * freqs, pos_c[:, None] * freqs], axis=-1)
    return jnp.tile(jnp.cos(ang), (1, GROUP_HEADS)), jnp.tile(jnp.sin(ang), (1, GROUP_HEADS))


def _bias_tables(rpb):
    qc = jnp.arange(GRID_W)
    kc = jnp.arange(GRID_W)
    cs = jnp.clip(qc - WIN_W // 2, 0, GRID_W - WIN_W)
    inband = (kc[None, :] >= cs[:, None]) & (kc[None, :] < cs[:, None] + WIN_W)
    dc = jnp.clip(kc[None, :] - qc[:, None] + (WIN_W - 1), 0, 2 * WIN_W - 2)
    dr = jnp.arange(WIN_H)[:, None] + jnp.arange(WIN_H)[None, :]
    t = rpb[:, dr][:, :, :, dc]
    t = jnp.where(inband, t, NEG_BIAS)
    t = t.reshape(HEAD_GROUPS, GROUP_HEADS, WIN_H, WIN_H, GRID_W, GRID_W)
    t = t.transpose(2, 0, 1, 4, 3, 5)
    return t.reshape(WIN_H, HEAD_GROUPS, GROUP_HEADS * GRID_W, WIN_H * GRID_W).astype(F32)


def kernel(x, c, ctx, c_ctx, w_ada, b_ada, norm_g, w_ff1_up, w_ff1_down, w_in, b_gate, rpb,
           ln_v_g, ln_v_b, w_s, b_s, w_pa, w_pb, w_o, w_ff2_up, w_ff2_down, final_g):
    batch, n, _ = x.shape
    ctx_len = ctx.shape[1]
    depth = w_ada.shape[0]
    assert n % (8 * GRID_W) == 0 and ctx_len % SG_CHUNK == 0 and batch + 1 <= COND_ROWS
    tm_x, tm_c = 512, ctx_len

    cond = jnp.zeros((COND_ROWS, D_MODEL), F32).at[:batch].set(c).at[batch].set(c_ctx)
    mods = _ada(cond, w_ada, b_ada).reshape(depth, COND_ROWS, N_MOD, D_MODEL)

    cos_x, sin_x = _rope_tables(n)
    cos_c = jnp.ones((ctx_len, ROPE_LANES), F32)
    sin_c = jnp.zeros((ctx_len, ROPE_LANES), F32)

    for l in range(depth):
        last = l == depth - 1
        mx = mods[l, :batch]
        mc = jnp.broadcast_to(mods[l, batch:batch + 1], (batch, N_MOD, D_MODEL))
        ff1 = (w_ff1_up[l].astype(BF16), w_ff1_down[l].astype(BF16))
        ff2 = (w_ff2_up[l].astype(BF16), w_ff2_down[l].astype(BF16))
        w_in_l = w_in[l].astype(BF16)
        bias = _bias_tables(rpb[l])
        bs_full = jnp.repeat(b_s[l].T, SG_WIDTH // SG_GROUPS, axis=1)
        w_s_l = w_s[l].astype(BF16)
        w_pa_l, w_pb_l, w_o_l = w_pa[l].astype(BF16), w_pb[l].astype(BF16), w_o[l].astype(BF16)

        x = _ffn(x, mx, 0, norm_g[l, 0], *ff1, final_g, final=False, tm=tm_x)
        ctx = _ffn(ctx, mc, 0, norm_g[l, 0], *ff1, final_g, final=False, tm=tm_c)

        qx, kx, vx, ux, vnx, glx = _inproj(x, mx, norm_g[l, 1], w_in_l, cos_x, sin_x,
                                           ln_v_g[l], ln_v_b[l], tm=tm_x)
        qc, kc, vc, uc, vnc, glc = _inproj(ctx, mc, norm_g[l, 1], w_in_l, cos_c, sin_c,
                                           ln_v_g[l], ln_v_b[l], tm=tm_c)
        oa = _attn(qx, kx, vx, kc, vc, bias)
        x = _merge(x, oa, ux, vnx, glx, mx, w_s_l, bs_full, b_gate[l], w_pa_l, w_pb_l, w_o_l, tm=tm_x)

        if not last:
            oac = _ctx_attn(qc, kc, vc)
            ctx = _merge(ctx, oac, uc, vnc, glc, mc, w_s_l, bs_full, b_gate[l], w_pa_l, w_pb_l, w_o_l,
                         tm=tm_c)
            ctx = _ffn(ctx, mc, 6, norm_g[l, 2], *ff2, final_g, final=False, tm=tm_c)

        x = _ffn(x, mx, 6, norm_g[l, 2], *ff2, final_g, final=last, tm=tm_x)

    return x
```

```python
import functools

import jax
import jax.numpy as jnp
import numpy as np
from jax import lax
from jax.experimental import pallas as pl
from jax.experimental.pallas import tpu as pltpu

D_MODEL = 1024
GRID_W = 64
NA_HEADS = 8
HEAD_DIM = 64
NA_WIDTH = NA_HEADS * HEAD_DIM
WIN_H = 8
WIN_W = 16
SG_GROUPS = 4
SG_CHUNK = 128
SG_WIDTH = 512
D_FF = 2816
ROPE_THETA = 10000.0
EPS = 1e-6
N_MOD = 9
G0 = 3 * NA_WIDTH + 2 * SG_WIDTH
IN_COLS = G0 + 2 * D_MODEL

HEAD_GROUPS = 2
GROUP_HEADS = NA_HEADS // HEAD_GROUPS
GROUP_LANES = GROUP_HEADS * HEAD_DIM
ROPE_LANES = GROUP_LANES // 2
FF_CHUNK = 256
N_FF_CHUNKS = D_FF // FF_CHUNK
ADA_COLS = 1024
COND_ROWS = 8
NEG_BIAS = -1e30
VMEM_LIMIT = 56 * 1024 * 1024

BF16 = jnp.bfloat16
F32 = jnp.float32


def _dot(a, b):
    return jnp.dot(a, b, preferred_element_type=F32)


def _dot_nt(a, b):
    return lax.dot_general(a, b, (((1,), (1,)), ((), ())), preferred_element_type=F32)


def _rms_modulate(x, g, shift, scale):
    ms = jnp.mean(x * x, axis=-1, keepdims=True)
    return x * lax.rsqrt(ms + EPS) * (g * (1.0 + scale)) + shift


def _const_spec(shape):
    zeros = (0,) * len(shape)
    return pl.BlockSpec(shape, lambda *_: zeros, pipeline_mode=pl.Buffered(1))


def _layer_spec(shape, layer):
    index = (layer,) + (0,) * len(shape)
    return pl.BlockSpec((None,) + tuple(shape), lambda *_: index, pipeline_mode=pl.Buffered(1))


def _ada_kernel(cond_ref, w_ref, b_ref, out_ref):
    c = cond_ref[...]
    s = c * jax.nn.sigmoid(c)
    w = w_ref[0]
    s_hi = s.astype(BF16)
    s_lo = (s - s_hi.astype(F32)).astype(BF16)
    w_hi = w.astype(BF16)
    w_lo = (w - w_hi.astype(F32)).astype(BF16)
    acc = _dot(s_hi, w_hi) + _dot(s_hi, w_lo) + _dot(s_lo, w_hi)
    out_ref[0] = acc + b_ref[0]


def _ada(cond, w_ada, b_ada):
    depth = w_ada.shape[0]
    cols = w_ada.shape[2]
    return pl.pallas_call(
        _ada_kernel,
        grid=(depth, cols // ADA_COLS),
        in_specs=[
            pl.BlockSpec((COND_ROWS, D_MODEL), lambda l, j: (0, 0)),
            pl.BlockSpec((1, D_MODEL, ADA_COLS), lambda l, j: (l, 0, j)),
            pl.BlockSpec((1, 1, ADA_COLS), lambda l, j: (l, 0, j)),
        ],
        out_specs=pl.BlockSpec((1, COND_ROWS, ADA_COLS), lambda l, j: (l, 0, j)),
        out_shape=jax.ShapeDtypeStruct((depth, COND_ROWS, cols), F32),
        compiler_params=pltpu.CompilerParams(
            dimension_semantics=("parallel", "parallel"), vmem_limit_bytes=VMEM_LIMIT),
        name="ada",
    )(cond, w_ada, b_ada.reshape(depth, 1, cols))


def _ffn_body(x_ref, mod_ref, g_ref, wup_ref, wd_ref, fg_ref, out_ref, h_scr, acc_scr, *, mod_row, final):
    x = x_ref[...]
    shift = mod_ref[mod_row:mod_row + 1, :]
    scale = mod_ref[mod_row + 1:mod_row + 2, :]
    gate = mod_ref[mod_row + 2:mod_row + 3, :]
    h_scr[...] = _rms_modulate(x, g_ref[...], shift, scale).astype(BF16)

    for c in range(N_FF_CHUNKS):
        cols = slice(c * FF_CHUNK, (c + 1) * FF_CHUNK)
        gate_cols = slice(D_FF + c * FF_CHUNK, D_FF + (c + 1) * FF_CHUNK)
        h = h_scr[...]
        a = _dot(h, wup_ref[:, cols])
        b = _dot(h, wup_ref[:, gate_cols])
        act = (a * jax.nn.sigmoid(a) * b).astype(BF16)
        y = _dot(act, wd_ref[cols, :])
        if c == 0:
            acc_scr[...] = y
        else:
            acc_scr[...] += y
    o = x_ref[...] + (0.5 * gate) * acc_scr[...]
    if final:
        ms = jnp.mean(o * o, axis=-1, keepdims=True)
        o = o * lax.rsqrt(ms + EPS) * fg_ref[...]
    out_ref[...] = o


def _ffn_kernel(x_ref, mod_ref, g_ref, wup_ref, wd_ref, out_ref, h_scr, acc_scr):
    _ffn_body(x_ref, mod_ref, g_ref, wup_ref, wd_ref, None, out_ref, h_scr, acc_scr, mod_row=0, final=False)


def _ffn(x, mod, g, w_up, w_down, layer, *, tm):
    batch, n, _ = x.shape
    return pl.pallas_call(
        _ffn_kernel,
        grid=(batch, n // tm),
        in_specs=[
            pl.BlockSpec((None, tm, D_MODEL), lambda b, i: (b, i, 0)),
            pl.BlockSpec((None, N_MOD, D_MODEL), lambda b, i: (b, 0, 0)),
            _const_spec((1, D_MODEL)),
            _layer_spec((D_MODEL, 2 * D_FF), layer),
            _layer_spec((D_FF, D_MODEL), layer),
        ],
        out_specs=pl.BlockSpec((None, tm, D_MODEL), lambda b, i: (b, i, 0)),
        out_shape=jax.ShapeDtypeStruct(x.shape, F32),
        scratch_shapes=[pltpu.VMEM((tm, D_MODEL), BF16), pltpu.VMEM((tm, D_MODEL), F32)],
        compiler_params=pltpu.CompilerParams(
            dimension_semantics=("parallel", "parallel"), vmem_limit_bytes=VMEM_LIMIT),
        name="ffn",
    )(x, mod, g.reshape(1, D_MODEL), w_up, w_down)


def _deinterleave_matrix():
    old = lax.broadcasted_iota(jnp.int32, (GROUP_LANES, GROUP_LANES), 0)
    new = lax.broadcasted_iota(jnp.int32, (GROUP_LANES, GROUP_LANES), 1)
    half = new // ROPE_LANES
    head = (new % ROPE_LANES) // (HEAD_DIM // 2)
    pair = new % (HEAD_DIM // 2)
    return jnp.where(old == HEAD_DIM * head + 2 * pair + half, 1.0, 0.0).astype(BF16)


def _inproj_kernel(x_ref, mod_ref, g_ref, w_ref, cos_ref, sin_ref, lng_ref, lnb_ref,
                   q_ref, k_ref, v_ref, u_ref, vn_ref, gl_ref, wqk_scr):
    @pl.when((pl.program_id(0) == 0) & (pl.program_id(1) == 0))
    def _():
        perm = _deinterleave_matrix()
        for t in range(2 * HEAD_GROUPS):
            cols = slice(t * GROUP_LANES, (t + 1) * GROUP_LANES)
            wqk_scr[:, cols] = _dot(w_ref[:, cols], perm).astype(BF16)

    x = x_ref[...]
    h = _rms_modulate(x, g_ref[...], mod_ref[3:4, :], mod_ref[4:5, :]).astype(BF16)
    cos = cos_ref[...]
    sin = sin_ref[...]

    def rope_store(t, out_ref, mult):
        for g in range(HEAD_GROUPS):
            lo = g * GROUP_LANES
            e = t[:, lo:lo + ROPE_LANES]
            o = t[:, lo + ROPE_LANES:lo + GROUP_LANES]
            out_ref[:, lo:lo + ROPE_LANES] = ((e * cos - o * sin) * mult).astype(BF16)
            out_ref[:, lo + ROPE_LANES:lo + GROUP_LANES] = ((e * sin + o * cos) * mult).astype(BF16)

    w = NA_WIDTH
    rope_store(_dot(h, wqk_scr[:, 0:w]), q_ref, HEAD_DIM ** -0.5)
    rope_store(_dot(h, wqk_scr[:, w:2 * w]), k_ref, 1.0)
    v_ref[...] = _dot(h, w_ref[:, 2 * w:3 * w]).astype(BF16)
    u0 = 3 * w
    u_ref[...] = jax.nn.gelu(_dot(h, w_ref[:, u0:u0 + SG_WIDTH])).astype(BF16)
    vs = jax.nn.gelu(_dot(h, w_ref[:, u0 + SG_WIDTH:G0]))
    mu = jnp.mean(vs, axis=-1, keepdims=True)
    dv = vs - mu
    var = jnp.mean(dv * dv, axis=-1, keepdims=True)
    vn_ref[...] = (dv * lax.rsqrt(var + EPS) * lng_ref[...] + lnb_ref[...]).astype(BF16)
    gl_ref[...] = _dot(h, w_ref[:, G0:IN_COLS])


def _inproj(x, mod, g, w_in, layer, cos, sin, ln_g, ln_b, *, tm):
    batch, n, _ = x.shape
    tok = lambda width, dt: jax.ShapeDtypeStruct((batch, n, width), dt)
    tok_spec = lambda width: pl.BlockSpec((None, tm, width), lambda b, i: (b, i, 0))
    return pl.pallas_call(
        _inproj_kernel,
        grid=(batch, n // tm),
        in_specs=[
            tok_spec(D_MODEL),
            pl.BlockSpec((None, N_MOD, D_MODEL), lambda b, i: (b, 0, 0)),
            _const_spec((1, D_MODEL)),
            _layer_spec((D_MODEL, IN_COLS), layer),
            pl.BlockSpec((tm, ROPE_LANES), lambda b, i: (i, 0)),
            pl.BlockSpec((tm, ROPE_LANES), lambda b, i: (i, 0)),
            _const_spec((1, SG_WIDTH)),
            _const_spec((1, SG_WIDTH)),
        ],
        out_specs=[tok_spec(NA_WIDTH), tok_spec(NA_WIDTH), tok_spec(NA_WIDTH),
                   tok_spec(SG_WIDTH), tok_spec(SG_WIDTH), tok_spec(2 * D_MODEL)],
        out_shape=[tok(NA_WIDTH, BF16), tok(NA_WIDTH, BF16), tok(NA_WIDTH, BF16),
                   tok(SG_WIDTH, BF16), tok(SG_WIDTH, BF16), tok(2 * D_MODEL, F32)],
        scratch_shapes=[pltpu.VMEM((D_MODEL, 2 * NA_WIDTH), BF16)],
        compiler_params=pltpu.CompilerParams(
            dimension_semantics=("arbitrary", "arbitrary"), vmem_limit_bytes=VMEM_LIMIT),
        name="inproj",
    )(x, mod, g.reshape(1, D_MODEL), w_in, cos, sin,
      ln_g.reshape(1, SG_WIDTH), ln_b.reshape(1, SG_WIDTH))


def _head_row_mask():
    row_head = lax.broadcasted_iota(jnp.int32, (GROUP_LANES, GROUP_LANES), 0) // GRID_W
    lane_head = (lax.broadcasted_iota(jnp.int32, (GROUP_LANES, GROUP_LANES), 1) % ROPE_LANES) // (HEAD_DIM // 2)
    return row_head == lane_head


def _unstack_heads(o, rows):
    lane_head = lax.broadcasted_iota(jnp.int32, (rows, GROUP_LANES), 1) // HEAD_DIM
    out = jnp.zeros((rows, GROUP_LANES), F32)
    for j in range(GROUP_HEADS):
        out = out + jnp.where(lane_head == j, o[j * rows:(j + 1) * rows, :], 0.0)
    return out


def _attn_kernel(q_ref, k_ref, v_ref, kc_ref, vc_ref, bias_ref, o_ref, *, rows, block_rows, halo_rows):
    r0 = pl.program_id(1) * block_rows
    halo_start = jnp.clip(r0 - WIN_H // 2, 0, rows - halo_rows)
    qmask = _head_row_mask()

    def row_body(ri, carry):
        r = r0 + ri
        rs = jnp.clip(r - WIN_H // 2, 0, rows - WIN_H)
        koff = pl.multiple_of((rs - halo_start) * GRID_W, GRID_W)
        d0 = rs - r + (WIN_H - 1)
        qoff = ri * GRID_W
        for g in range(HEAD_GROUPS):
            lanes = slice(g * GROUP_LANES, (g + 1) * GROUP_LANES)
            qg = q_ref[pl.ds(qoff, GRID_W), lanes]
            qs = jnp.where(qmask, jnp.concatenate([qg] * GROUP_HEADS, axis=0), jnp.zeros((), BF16))
            kw = k_ref[pl.ds(koff, WIN_H * GRID_W), lanes]
            bias = jnp.concatenate([bias_ref[g, d0 + 2 * p] for p in range(WIN_H // 2)], axis=-1)
            s_nb = _dot_nt(qs, kw) + bias
            s_c = _dot_nt(qs, kc_ref[:, lanes])
            m = jnp.maximum(jnp.max(s_nb, axis=-1, keepdims=True),
                            jnp.max(s_c, axis=-1, keepdims=True))
            p_nb = jnp.exp(s_nb - m)
            p_c = jnp.exp(s_c - m)
            denom = jnp.sum(p_nb, axis=-1, keepdims=True) + jnp.sum(p_c, axis=-1, keepdims=True)
            vw = v_ref[pl.ds(koff, WIN_H * GRID_W), lanes]
            o = _dot(p_nb.astype(BF16), vw) + _dot(p_c.astype(BF16), vc_ref[:, lanes])
            o = o / denom
            o_ref[pl.ds(qoff, GRID_W), lanes] = _unstack_heads(o, GRID_W).astype(BF16)
        return carry

    for ri in range(block_rows):
        row_body(ri, 0)


def _attn(q, k, v, kc, vc, bias):
    batch, n, _ = q.shape
    ctx_len = kc.shape[1]
    rows = n // GRID_W
    block_rows = 8
    halo_rows = block_rows + WIN_H
    tm = block_rows * GRID_W

    def halo_map(b, i):
        start = jnp.clip(i * block_rows - WIN_H // 2, 0, rows - halo_rows)
        return (b, start * GRID_W, 0)

    halo_spec = pl.BlockSpec((None, pl.Element(halo_rows * GRID_W), pl.Element(NA_WIDTH)), halo_map)
    kern = functools.partial(_attn_kernel, rows=rows, block_rows=block_rows, halo_rows=halo_rows)
    return pl.pallas_call(
        kern,
        grid=(batch, rows // block_rows),
        in_specs=[
            pl.BlockSpec((None, tm, NA_WIDTH), lambda b, i: (b, i, 0)),
            halo_spec,
            halo_spec,
            pl.BlockSpec((None, ctx_len, NA_WIDTH), lambda b, i: (b, 0, 0)),
            pl.BlockSpec((None, ctx_len, NA_WIDTH), lambda b, i: (b, 0, 0)),
            _const_spec(bias.shape),
        ],
        out_specs=pl.BlockSpec((None, tm, NA_WIDTH), lambda b, i: (b, i, 0)),
        out_shape=jax.ShapeDtypeStruct((batch, n, NA_WIDTH), BF16),
        compiler_params=pltpu.CompilerParams(
            dimension_semantics=("parallel", "parallel"), vmem_limit_bytes=VMEM_LIMIT),
        name="nbr_attn",
    )(q, k, v, kc, vc, bias)


def _ctx_attn_kernel(q_ref, k_ref, v_ref, o_ref):
    n = q_ref.shape[0]
    lane_head_qk = (lax.broadcasted_iota(jnp.int32, (n, GROUP_LANES), 1) % ROPE_LANES) // (HEAD_DIM // 2)
    lane_head_v = lax.broadcasted_iota(jnp.int32, (n, GROUP_LANES), 1) // HEAD_DIM
    for g in range(HEAD_GROUPS):
        lanes = slice(g * GROUP_LANES, (g + 1) * GROUP_LANES)
        qg = q_ref[:, lanes]
        kg = k_ref[:, lanes]
        vg = v_ref[:, lanes]
        out = jnp.zeros((n, GROUP_LANES), F32)
        for j in range(GROUP_HEADS):
            qj = jnp.where(lane_head_qk == j, qg, jnp.zeros((), BF16))
            s = _dot_nt(qj, kg)
            p = jnp.exp(s - jnp.max(s, axis=-1, keepdims=True))
            o = _dot(p.astype(BF16), vg) / jnp.sum(p, axis=-1, keepdims=True)
            out = out + jnp.where(lane_head_v == j, o, 0.0)
        o_ref[:, lanes] = out.astype(BF16)


def _ctx_attn(q, k, v):
    batch, n, width = q.shape
    spec = pl.BlockSpec((None, n, width), lambda b: (b, 0, 0))
    return pl.pallas_call(
        _ctx_attn_kernel,
        grid=(batch,),
        in_specs=[spec, spec, spec],
        out_specs=spec,
        out_shape=jax.ShapeDtypeStruct(q.shape, BF16),
        compiler_params=pltpu.CompilerParams(dimension_semantics=("parallel",)),
        name="ctx_attn",
    )(q, k, v)


def _merge_ffn_kernel(x_ref, oa_ref, u_ref, vn_ref, gl_ref, mod_ref, ws_ref, bs_ref, bg_ref,
                      wpa_ref, wpb_ref, wo_ref, g_ref, wup_ref, wd_ref, fg_ref, out_ref,
                      ob_scr, h_scr, acc_scr, *, final):
    tm = x_ref.shape[0]
    gw = SG_WIDTH // SG_GROUPS
    for n in range(tm // SG_CHUNK):
        rows = slice(n * SG_CHUNK, (n + 1) * SG_CHUNK)
        for g in range(SG_GROUPS):
            lanes = slice(g * gw, (g + 1) * gw)
            s = _dot(ws_ref[g], vn_ref[rows, lanes]) + bs_ref[:, lanes]
            ob_scr[rows, lanes] = (u_ref[rows, lanes].astype(F32) * s).astype(BF16)
    a = _dot(oa_ref[...], wpa_ref[...])
    b = _dot(ob_scr[...], wpb_ref[...])
    gates = jax.nn.sigmoid(gl_ref[...] + bg_ref[...])
    mixed = (gates[:, :D_MODEL] * a + gates[:, D_MODEL:] * b).astype(BF16)
    out_ref[...] = x_ref[...] + mod_ref[5:6, :] * _dot(mixed, wo_ref[...])
    _ffn_body(out_ref, mod_ref, g_ref, wup_ref, wd_ref, fg_ref, out_ref, h_scr, acc_scr,
              mod_row=6, final=final)


def _merge_ffn(x, oa, u, vn, gl, mod, w_s, bs_full, b_gate, w_pa, w_pb, w_o, g, w_up, w_down, layer,
               final_g, *, final, tm):
    batch, n, _ = x.shape
    tok_spec = lambda width: pl.BlockSpec((None, tm, width), lambda b, i: (b, i, 0))
    return pl.pallas_call(
        functools.partial(_merge_ffn_kernel, final=final),
        grid=(batch, n // tm),
        in_specs=[
            tok_spec(D_MODEL), tok_spec(NA_WIDTH), tok_spec(SG_WIDTH), tok_spec(SG_WIDTH),
            tok_spec(2 * D_MODEL),
            pl.BlockSpec((None, N_MOD, D_MODEL), lambda b, i: (b, 0, 0)),
            _layer_spec(w_s.shape[1:], layer),
            _const_spec(bs_full.shape),
            _const_spec((1, 2 * D_MODEL)),
            _layer_spec(w_pa.shape[1:], layer),
            _layer_spec(w_pb.shape[1:], layer),
            _layer_spec(w_o.shape[1:], layer),
            _const_spec((1, D_MODEL)),
            _layer_spec((D_MODEL, 2 * D_FF), layer),
            _layer_spec((D_FF, D_MODEL), layer),
            _const_spec((1, D_MODEL)),
        ],
        out_specs=tok_spec(D_MODEL),
        out_shape=jax.ShapeDtypeStruct(x.shape, F32),
        scratch_shapes=[pltpu.VMEM((tm, SG_WIDTH), BF16), pltpu.VMEM((tm, D_MODEL), BF16),
                        pltpu.VMEM((tm, D_MODEL), F32)],
        compiler_params=pltpu.CompilerParams(
            dimension_semantics=("parallel", "parallel"), vmem_limit_bytes=VMEM_LIMIT),
        name="merge_ffn_final" if final else "merge_ffn",
    )(x, oa, u, vn, gl, mod, w_s, bs_full, b_gate.reshape(1, 2 * D_MODEL), w_pa, w_pb, w_o,
      g.reshape(1, D_MODEL), w_up, w_down, final_g.reshape(1, D_MODEL))


def _rope_tables(n):
    rows = n // GRID_W
    n_freq = HEAD_DIM // 4
    freqs = ROPE_THETA ** (-jnp.arange(n_freq, dtype=F32) / n_freq)
    ang_r = jnp.arange(rows, dtype=F32)[:, None] * freqs
    ang_c = jnp.arange(GRID_W, dtype=F32)[:, None] * freqs

    def table(fn):
        full = (rows, GRID_W, GROUP_HEADS, n_freq)
        by_row = jnp.broadcast_to(fn(ang_r)[:, None, None, :], full)
        by_col = jnp.broadcast_to(fn(ang_c)[None, :, None, :], full)
        return jnp.concatenate([by_row, by_col], axis=-1).reshape(n, ROPE_LANES)

    return table(jnp.cos), table(jnp.sin)


def _bias_tables(rpb):
    qc = np.arange(GRID_W)[:, None]
    kc = np.arange(GRID_W)[None, :]
    cs = np.clip(qc - WIN_W // 2, 0, GRID_W - WIN_W)
    inband = (kc >= cs) & (kc < cs + WIN_W)
    dc = kc - qc + (WIN_W - 1)
    onehot = (inband[None] & (dc[None] == np.arange(2 * WIN_W - 1)[:, None, None])).astype(np.float32)
    m = jnp.einsum("hdc,cqk->hdqk", rpb, onehot, precision=lax.Precision.HIGHEST)
    m = jnp.where(inband, m, NEG_BIAS)
    n_dr = 2 * WIN_H - 1
    m = m.reshape(HEAD_GROUPS, GROUP_HEADS, n_dr, GRID_W, GRID_W).transpose(0, 2, 1, 3, 4)
    m = m.reshape(HEAD_GROUPS, n_dr, GROUP_HEADS * GRID_W, GRID_W)
    return jnp.concatenate([m[:, :-1], m[:, 1:]], axis=-1)


def kernel(x, c, ctx, c_ctx, w_ada, b_ada, norm_g, w_ff1_up, w_ff1_down, w_in, b_gate, rpb,
           ln_v_g, ln_v_b, w_s, b_s, w_pa, w_pb, w_o, w_ff2_up, w_ff2_down, final_g):
    batch, n, _ = x.shape
    ctx_len = ctx.shape[1]
    depth = w_ada.shape[0]
    assert n % (8 * GRID_W) == 0 and ctx_len % SG_CHUNK == 0 and batch + 1 <= COND_ROWS
    tm_x, tm_c = 512, ctx_len

    cond = jnp.zeros((COND_ROWS, D_MODEL), F32).at[:batch].set(c).at[batch].set(c_ctx)
    mods = _ada(cond, w_ada, b_ada).reshape(depth, COND_ROWS, N_MOD, D_MODEL)

    cos_x, sin_x = _rope_tables(n)
    cos_c = jnp.ones((ctx_len, ROPE_LANES), F32)
    sin_c = jnp.zeros((ctx_len, ROPE_LANES), F32)

    ff1 = (w_ff1_up.astype(BF16), w_ff1_down.astype(BF16))
    ff2 = (w_ff2_up.astype(BF16), w_ff2_down.astype(BF16))
    w_in_b = w_in.astype(BF16)
    mix_w = (w_pa.astype(BF16), w_pb.astype(BF16), w_o.astype(BF16))
    w_s_b = w_s.astype(BF16)

    for l in range(depth):
        last = l == depth - 1
        mx = mods[l, :batch]
        mc = jnp.broadcast_to(mods[l, batch:batch + 1], (batch, N_MOD, D_MODEL))
        bias = _bias_tables(rpb[l])
        bs_full = jnp.repeat(b_s[l].T, SG_WIDTH // SG_GROUPS, axis=1)

        x = _ffn(x, mx, norm_g[l, 0], *ff1, l, tm=tm_x)
        ctx = _ffn(ctx, mc, norm_g[l, 0], *ff1, l, tm=tm_c)

        qx, kx, vx, ux, vnx, glx = _inproj(x, mx, norm_g[l, 1], w_in_b, l, cos_x, sin_x,
                                           ln_v_g[l], ln_v_b[l], tm=tm_x)
        qc, kc, vc, uc, vnc, glc = _inproj(ctx, mc, norm_g[l, 1], w_in_b, l, cos_c, sin_c,
                                           ln_v_g[l], ln_v_b[l], tm=tm_c)
        oa = _attn(qx, kx, vx, kc, vc, bias)
        x = _merge_ffn(x, oa, ux, vnx, glx, mx, w_s_b, bs_full, b_gate[l], *mix_w, norm_g[l, 2], *ff2, l,
                       final_g, final=last, tm=tm_x)
        if not last:
            oac = _ctx_attn(qc, kc, vc)
            ctx = _merge_ffn(ctx, oac, uc, vnc, glc, mc, w_s_b, bs_full, b_gate[l], *mix_w, norm_g[l, 2],
                             *ff2, l, final_g, final=False, tm=tm_c)

    return x
```

```python
import functools
import math

import jax
import jax.numpy as jnp
import numpy as np
from jax import lax
from jax.experimental import pallas as pl
from jax.experimental.pallas import tpu as pltpu

D_MODEL = 1024
GRID_W = 64
NA_HEADS = 8
HEAD_DIM = 64
NA_WIDTH = NA_HEADS * HEAD_DIM
WIN_H = 8
WIN_W = 16
SG_GROUPS = 4
SG_CHUNK = 128
SG_WIDTH = 512
D_FF = 2816
ROPE_THETA = 10000.0
EPS = 1e-6
N_MOD = 9
G0 = 3 * NA_WIDTH + 2 * SG_WIDTH
IN_COLS = G0 + 2 * D_MODEL

HEAD_GROUPS = 2
GROUP_HEADS = NA_HEADS // HEAD_GROUPS
GROUP_LANES = GROUP_HEADS * HEAD_DIM
ROPE_LANES = GROUP_LANES // 2
FF_CHUNK = 256
N_FF_CHUNKS = D_FF // FF_CHUNK
ADA_COLS = 2304
LOG2E = math.log2(math.e)
Q_SCALE = HEAD_DIM ** -0.5 * LOG2E
COND_ROWS = 8
NEG_BIAS = -1e30
VMEM_LIMIT = 56 * 1024 * 1024

BF16 = jnp.bfloat16
F32 = jnp.float32


def _dot(a, b):
    return jnp.dot(a, b, preferred_element_type=F32)


def _dot_nt(a, b):
    return lax.dot_general(a, b, (((1,), (1,)), ((), ())), preferred_element_type=F32)


def _rms_modulate(x, g, shift, scale):
    ms = jnp.mean(x * x, axis=-1, keepdims=True)
    return x * lax.rsqrt(ms + EPS) * (g * (1.0 + scale)) + shift


def _const_spec(shape):
    zeros = (0,) * len(shape)
    return pl.BlockSpec(shape, lambda *_: zeros, pipeline_mode=pl.Buffered(1))


def _layer_spec(shape, layer):
    index = (layer,) + (0,) * len(shape)
    return pl.BlockSpec((None,) + tuple(shape), lambda *_: index, pipeline_mode=pl.Buffered(1))


def _ada_kernel(cond_ref, w_ref, b_ref, out_ref):
    c = cond_ref[...]
    s = c * jax.nn.sigmoid(c)
    w = w_ref[0]
    s_hi = s.astype(BF16)
    s_lo = (s - s_hi.astype(F32)).astype(BF16)
    w_hi = w.astype(BF16)
    w_lo = (w - w_hi.astype(F32)).astype(BF16)
    acc = _dot(s_hi, w_hi) + _dot(s_hi, w_lo) + _dot(s_lo, w_hi)
    out_ref[0] = acc + b_ref[0]


def _ada(cond, w_ada, b_ada):
    depth = w_ada.shape[0]
    cols = w_ada.shape[2]
    return pl.pallas_call(
        _ada_kernel,
        grid=(depth, cols // ADA_COLS),
        in_specs=[
            pl.BlockSpec((COND_ROWS, D_MODEL), lambda l, j: (0, 0)),
            pl.BlockSpec((1, D_MODEL, ADA_COLS), lambda l, j: (l, 0, j)),
            pl.BlockSpec((1, 1, ADA_COLS), lambda l, j: (l, 0, j)),
        ],
        out_specs=pl.BlockSpec((1, COND_ROWS, ADA_COLS), lambda l, j: (l, 0, j)),
        out_shape=jax.ShapeDtypeStruct((depth, COND_ROWS, cols), F32),
        compiler_params=pltpu.CompilerParams(
            dimension_semantics=("parallel", "parallel"), vmem_limit_bytes=VMEM_LIMIT),
        name="ada",
    )(cond, w_ada, b_ada.reshape(depth, 1, cols))


def _ffn_body(x_ref, mod_ref, g_ref, wup_ref, wd_ref, fg_ref, out_ref, h_scr, acc_scr, *, mod_row, final):
    x = x_ref[...]
    shift = mod_ref[mod_row:mod_row + 1, :]
    scale = mod_ref[mod_row + 1:mod_row + 2, :]
    gate = mod_ref[mod_row + 2:mod_row + 3, :]
    h_scr[...] = _rms_modulate(x, g_ref[...], shift, scale).astype(BF16)

    for c in range(N_FF_CHUNKS):
        cols = slice(c * FF_CHUNK, (c + 1) * FF_CHUNK)
        gate_cols = slice(D_FF + c * FF_CHUNK, D_FF + (c + 1) * FF_CHUNK)
        h = h_scr[...]
        a = _dot(h, wup_ref[:, cols])
        b = _dot(h, wup_ref[:, gate_cols])
        act = (a * jax.nn.sigmoid(a) * b).astype(BF16)
        y = _dot(act, wd_ref[cols, :])
        if c == 0:
            acc_scr[...] = y
        else:
            acc_scr[...] += y
    o = x_ref[...] + (0.5 * gate) * acc_scr[...]
    if final:
        ms = jnp.mean(o * o, axis=-1, keepdims=True)
        o = o * lax.rsqrt(ms + EPS) * fg_ref[...]
    out_ref[...] = o


def _ffn_kernel(x_ref, mod_ref, g_ref, wup_ref, wd_ref, out_ref, h_scr, acc_scr):
    _ffn_body(x_ref, mod_ref, g_ref, wup_ref, wd_ref, None, out_ref, h_scr, acc_scr, mod_row=0, final=False)


def _ffn(x, mod, g, w_up, w_down, layer, *, tm):
    batch, n, _ = x.shape
    return pl.pallas_call(
        _ffn_kernel,
        grid=(batch, n // tm),
        in_specs=[
            pl.BlockSpec((None, tm, D_MODEL), lambda b, i: (b, i, 0)),
            pl.BlockSpec((None, N_MOD, D_MODEL), lambda b, i: (b, 0, 0)),
            _const_spec((1, D_MODEL)),
            _layer_spec((D_MODEL, 2 * D_FF), layer),
            _layer_spec((D_FF, D_MODEL), layer),
        ],
        out_specs=pl.BlockSpec((None, tm, D_MODEL), lambda b, i: (b, i, 0)),
        out_shape=jax.ShapeDtypeStruct(x.shape, F32),
        scratch_shapes=[pltpu.VMEM((tm, D_MODEL), BF16), pltpu.VMEM((tm, D_MODEL), F32)],
        compiler_params=pltpu.CompilerParams(
            dimension_semantics=("parallel", "parallel"), vmem_limit_bytes=VMEM_LIMIT),
        name="ffn",
    )(x, mod, g.reshape(1, D_MODEL), w_up, w_down)


def _deinterleave_matrix():
    old = lax.broadcasted_iota(jnp.int32, (GROUP_LANES, GROUP_LANES), 0)
    new = lax.broadcasted_iota(jnp.int32, (GROUP_LANES, GROUP_LANES), 1)
    half = new // ROPE_LANES
    head = (new % ROPE_LANES) // (HEAD_DIM // 2)
    pair = new % (HEAD_DIM // 2)
    return jnp.where(old == HEAD_DIM * head + 2 * pair + half, 1.0, 0.0).astype(BF16)


def _inproj_kernel(x_ref, mod_ref, g_ref, w_ref, rcos_ref, ccos_ref, rsin_ref, csin_ref, lng_ref, lnb_ref,
                   q_ref, k_ref, v_ref, u_ref, vn_ref, gl_ref, wqk_scr):
    @pl.when((pl.program_id(0) == 0) & (pl.program_id(1) == 0))
    def _():
        perm = _deinterleave_matrix()
        for t in range(2 * HEAD_GROUPS):
            cols = slice(t * GROUP_LANES, (t + 1) * GROUP_LANES)
            wqk_scr[:, cols] = _dot(w_ref[:, cols], perm).astype(BF16)

    x = x_ref[...]
    h = _rms_modulate(x, g_ref[...], mod_ref[3:4, :], mod_ref[4:5, :]).astype(BF16)

    def token_table(by_row_ref, by_col_ref):
        by_col = by_col_ref[...]
        return jnp.concatenate([by_row_ref[r:r + 1, :] + by_col for r in range(by_row_ref.shape[0])], axis=0)

    cos = token_table(rcos_ref, ccos_ref)
    sin = token_table(rsin_ref, csin_ref)

    def rope_store(t, out_ref, mult):
        for g in range(HEAD_GROUPS):
            lo = g * GROUP_LANES
            e = t[:, lo:lo + ROPE_LANES]
            o = t[:, lo + ROPE_LANES:lo + GROUP_LANES]
            out_ref[:, lo:lo + ROPE_LANES] = ((e * cos - o * sin) * mult).astype(BF16)
            out_ref[:, lo + ROPE_LANES:lo + GROUP_LANES] = ((e * sin + o * cos) * mult).astype(BF16)

    w = NA_WIDTH
    rope_store(_dot(h, wqk_scr[:, 0:w]), q_ref, Q_SCALE)
    rope_store(_dot(h, wqk_scr[:, w:2 * w]), k_ref, 1.0)
    v_ref[...] = _dot(h, w_ref[:, 2 * w:3 * w]).astype(BF16)
    u0 = 3 * w
    u_ref[...] = jax.nn.gelu(_dot(h, w_ref[:, u0:u0 + SG_WIDTH])).astype(BF16)
    vs = jax.nn.gelu(_dot(h, w_ref[:, u0 + SG_WIDTH:G0]))
    mu = jnp.mean(vs, axis=-1, keepdims=True)
    dv = vs - mu
    var = jnp.mean(dv * dv, axis=-1, keepdims=True)
    vn_ref[...] = (dv * lax.rsqrt(var + EPS) * lng_ref[...] + lnb_ref[...]).astype(BF16)
    gl_ref[...] = _dot(h, w_ref[:, G0:IN_COLS])


def _inproj(x, mod, g, w_in, layer, rope, ln_g, ln_b, *, tm):
    batch, n, _ = x.shape
    tile_rows = tm // GRID_W
    by_row_spec = pl.BlockSpec((tile_rows, ROPE_LANES), lambda b, i: (i, 0))
    by_col_spec = _const_spec((GRID_W, ROPE_LANES))
    tok = lambda width, dt: jax.ShapeDtypeStruct((batch, n, width), dt)
    tok_spec = lambda width: pl.BlockSpec((None, tm, width), lambda b, i: (b, i, 0))
    return pl.pallas_call(
        _inproj_kernel,
        grid=(batch, n // tm),
        in_specs=[
            tok_spec(D_MODEL),
            pl.BlockSpec((None, N_MOD, D_MODEL), lambda b, i: (b, 0, 0)),
            _const_spec((1, D_MODEL)),
            _layer_spec((D_MODEL, IN_COLS), layer),
            by_row_spec, by_col_spec, by_row_spec, by_col_spec,
            _const_spec((1, SG_WIDTH)),
            _const_spec((1, SG_WIDTH)),
        ],
        out_specs=[tok_spec(NA_WIDTH), tok_spec(NA_WIDTH), tok_spec(NA_WIDTH),
                   tok_spec(SG_WIDTH), tok_spec(SG_WIDTH), tok_spec(2 * D_MODEL)],
        out_shape=[tok(NA_WIDTH, BF16), tok(NA_WIDTH, BF16), tok(NA_WIDTH, BF16),
                   tok(SG_WIDTH, BF16), tok(SG_WIDTH, BF16), tok(2 * D_MODEL, F32)],
        scratch_shapes=[pltpu.VMEM((D_MODEL, 2 * NA_WIDTH), BF16)],
        compiler_params=pltpu.CompilerParams(
            dimension_semantics=("arbitrary", "arbitrary"), vmem_limit_bytes=VMEM_LIMIT),
        name="inproj",
    )(x, mod, g.reshape(1, D_MODEL), w_in, *rope,
      ln_g.reshape(1, SG_WIDTH), ln_b.reshape(1, SG_WIDTH))


def _head_row_mask():
    row_head = lax.broadcasted_iota(jnp.int32, (GROUP_LANES, GROUP_LANES), 0) // GRID_W
    lane_head = (lax.broadcasted_iota(jnp.int32, (GROUP_LANES, GROUP_LANES), 1) % ROPE_LANES) // (HEAD_DIM // 2)
    return row_head == lane_head


def _unstack_heads(o, rows):
    lane_head = lax.broadcasted_iota(jnp.int32, (rows, GROUP_LANES), 1) // HEAD_DIM
    out = o[(GROUP_HEADS - 1) * rows:, :]
    for j in range(GROUP_HEADS - 2, -1, -1):
        out = jnp.where(lane_head == j, o[j * rows:(j + 1) * rows, :], out)
    return out


def _attn_kernel(q_ref, k_ref, v_ref, kc_ref, vc_ref, bias_ref, o_ref, *, rows, block_rows, halo_rows):
    r0 = pl.program_id(1) * block_rows
    halo_start = jnp.clip(r0 - WIN_H // 2, 0, rows - halo_rows)
    qmask = _head_row_mask()

    def row_body(ri):
        r = r0 + ri
        rs = jnp.clip(r - WIN_H // 2, 0, rows - WIN_H)
        koff = pl.multiple_of((rs - halo_start) * GRID_W, GRID_W)
        d0 = rs - r + (WIN_H - 1)
        qoff = ri * GRID_W
        for g in range(HEAD_GROUPS):
            lanes = slice(g * GROUP_LANES, (g + 1) * GROUP_LANES)
            qg = q_ref[pl.ds(qoff, GRID_W), lanes]
            qs = jnp.where(qmask, jnp.concatenate([qg] * GROUP_HEADS, axis=0), jnp.zeros((), BF16))
            kw = k_ref[pl.ds(koff, WIN_H * GRID_W), lanes]
            bias = jnp.concatenate([bias_ref[g, d0 + 2 * p] for p in range(WIN_H // 2)], axis=-1)
            s_nb = _dot_nt(qs, kw) + bias
            s_c = _dot_nt(qs, kc_ref[:, lanes])
            m = jnp.maximum(jnp.max(s_nb, axis=-1, keepdims=True),
                            jnp.max(s_c, axis=-1, keepdims=True))
            p_nb = jnp.exp2(s_nb - m)
            p_c = jnp.exp2(s_c - m)
            denom = jnp.sum(p_nb, axis=-1, keepdims=True) + jnp.sum(p_c, axis=-1, keepdims=True)
            vw = v_ref[pl.ds(koff, WIN_H * GRID_W), lanes]
            o = _dot(p_nb.astype(BF16), vw) + _dot(p_c.astype(BF16), vc_ref[:, lanes])
            o = o / denom
            o_ref[pl.ds(qoff, GRID_W), lanes] = _unstack_heads(o, GRID_W).astype(BF16)

    for ri in range(block_rows):
        row_body(ri)


def _attn(q, k, v, kc, vc, bias):
    batch, n, _ = q.shape
    ctx_len = kc.shape[1]
    rows = n // GRID_W
    block_rows = 8
    halo_rows = block_rows + WIN_H
    tm = block_rows * GRID_W

    def halo_map(b, i):
        start = jnp.clip(i * block_rows - WIN_H // 2, 0, rows - halo_rows)
        return (b, start * GRID_W, 0)

    halo_spec = pl.BlockSpec((None, pl.Element(halo_rows * GRID_W), pl.Element(NA_WIDTH)), halo_map)
    kern = functools.partial(_attn_kernel, rows=rows, block_rows=block_rows, halo_rows=halo_rows)
    return pl.pallas_call(
        kern,
        grid=(batch, rows // block_rows),
        in_specs=[
            pl.BlockSpec((None, tm, NA_WIDTH), lambda b, i: (b, i, 0)),
            halo_spec,
            halo_spec,
            pl.BlockSpec((None, ctx_len, NA_WIDTH), lambda b, i: (b, 0, 0)),
            pl.BlockSpec((None, ctx_len, NA_WIDTH), lambda b, i: (b, 0, 0)),
            _const_spec(bias.shape),
        ],
        out_specs=pl.BlockSpec((None, tm, NA_WIDTH), lambda b, i: (b, i, 0)),
        out_shape=jax.ShapeDtypeStruct((batch, n, NA_WIDTH), BF16),
        compiler_params=pltpu.CompilerParams(
            dimension_semantics=("parallel", "parallel"), vmem_limit_bytes=VMEM_LIMIT),
        name="nbr_attn",
    )(q, k, v, kc, vc, bias)


def _ctx_attn_kernel(q_ref, k_ref, v_ref, o_ref):
    n = q_ref.shape[0]
    lane_head_qk = (lax.broadcasted_iota(jnp.int32, (n, GROUP_LANES), 1) % ROPE_LANES) // (HEAD_DIM // 2)
    lane_head_v = lax.broadcasted_iota(jnp.int32, (n, GROUP_LANES), 1) // HEAD_DIM
    for g in range(HEAD_GROUPS):
        lanes = slice(g * GROUP_LANES, (g + 1) * GROUP_LANES)
        qg = q_ref[:, lanes]
        kg = k_ref[:, lanes]
        vg = v_ref[:, lanes]
        out = jnp.zeros((n, GROUP_LANES), F32)
        for j in range(GROUP_HEADS):
            qj = jnp.where(lane_head_qk == j, qg, jnp.zeros((), BF16))
            s = _dot_nt(qj, kg)
            p = jnp.exp2(s - jnp.max(s, axis=-1, keepdims=True))
            o = _dot(p.astype(BF16), vg) / jnp.sum(p, axis=-1, keepdims=True)
            out = out + jnp.where(lane_head_v == j, o, 0.0)
        o_ref[:, lanes] = out.astype(BF16)


def _ctx_attn(q, k, v):
    batch, n, width = q.shape
    spec = pl.BlockSpec((None, n, width), lambda b: (b, 0, 0))
    return pl.pallas_call(
        _ctx_attn_kernel,
        grid=(batch,),
        in_specs=[spec, spec, spec],
        out_specs=spec,
        out_shape=jax.ShapeDtypeStruct(q.shape, BF16),
        compiler_params=pltpu.CompilerParams(dimension_semantics=("parallel",)),
        name="ctx_attn",
    )(q, k, v)


def _merge_ffn_kernel(x_ref, oa_ref, u_ref, vn_ref, gl_ref, mod_ref, ws_ref, bs_ref, bg_ref,
                      wpa_ref, wpb_ref, wo_ref, g_ref, wup_ref, wd_ref, fg_ref, out_ref,
                      ob_scr, h_scr, acc_scr, *, final):
    tm = x_ref.shape[0]
    gw = SG_WIDTH // SG_GROUPS
    for n in range(tm // SG_CHUNK):
        rows = slice(n * SG_CHUNK, (n + 1) * SG_CHUNK)
        for g in range(SG_GROUPS):
            lanes = slice(g * gw, (g + 1) * gw)
            s = _dot(ws_ref[g], vn_ref[rows, lanes]) + bs_ref[:, lanes]
            ob_scr[rows, lanes] = (u_ref[rows, lanes].astype(F32) * s).astype(BF16)
    a = _dot(oa_ref[...], wpa_ref[...])
    b = _dot(ob_scr[...], wpb_ref[...])
    gates = jax.nn.sigmoid(gl_ref[...] + bg_ref[...])
    mixed = (gates[:, :D_MODEL] * a + gates[:, D_MODEL:] * b).astype(BF16)
    out_ref[...] = x_ref[...] + mod_ref[5:6, :] * _dot(mixed, wo_ref[...])
    _ffn_body(out_ref, mod_ref, g_ref, wup_ref, wd_ref, fg_ref, out_ref, h_scr, acc_scr,
              mod_row=6, final=final)


def _merge_ffn(x, oa, u, vn, gl, mod, w_s, bs_full, b_gate, w_pa, w_pb, w_o, g, w_up, w_down, layer,
               final_g, *, final, tm):
    batch, n, _ = x.shape
    tok_spec = lambda width: pl.BlockSpec((None, tm, width), lambda b, i: (b, i, 0))
    return pl.pallas_call(
        functools.partial(_merge_ffn_kernel, final=final),
        grid=(batch, n // tm),
        in_specs=[
            tok_spec(D_MODEL), tok_spec(NA_WIDTH), tok_spec(SG_WIDTH), tok_spec(SG_WIDTH),
            tok_spec(2 * D_MODEL),
            pl.BlockSpec((None, N_MOD, D_MODEL), lambda b, i: (b, 0, 0)),
            _layer_spec(w_s.shape[1:], layer),
            _const_spec(bs_full.shape),
            _const_spec((1, 2 * D_MODEL)),
            _layer_spec(w_pa.shape[1:], layer),
            _layer_spec(w_pb.shape[1:], layer),
            _layer_spec(w_o.shape[1:], layer),
            _const_spec((1, D_MODEL)),
            _layer_spec((D_MODEL, 2 * D_FF), layer),
            _layer_spec((D_FF, D_MODEL), layer),
            _const_spec((1, D_MODEL)),
        ],
        out_specs=tok_spec(D_MODEL),
        out_shape=jax.ShapeDtypeStruct(x.shape, F32),
        scratch_shapes=[pltpu.VMEM((tm, SG_WIDTH), BF16), pltpu.VMEM((tm, D_MODEL), BF16),
                        pltpu.VMEM((tm, D_MODEL), F32)],
        compiler_params=pltpu.CompilerParams(
            dimension_semantics=("parallel", "parallel"), vmem_limit_bytes=VMEM_LIMIT),
        name="merge_ffn_final" if final else "merge_ffn",
    )(x, oa, u, vn, gl, mod, w_s, bs_full, b_gate.reshape(1, 2 * D_MODEL), w_pa, w_pb, w_o,
      g.reshape(1, D_MODEL), w_up, w_down, final_g.reshape(1, D_MODEL))


def _rope_tables(rows):
    n_freq = HEAD_DIM // 4
    freqs = ROPE_THETA ** (-jnp.arange(n_freq, dtype=F32) / n_freq)
    ang_r = jnp.arange(rows, dtype=F32)[:, None] * freqs
    ang_c = jnp.arange(GRID_W, dtype=F32)[:, None] * freqs

    def lanes(row_part, col_part):
        return jnp.tile(jnp.concatenate([row_part, col_part], axis=-1), (1, GROUP_HEADS))

    zr, zc = jnp.zeros_like(ang_r), jnp.zeros_like(ang_c)
    return (lanes(jnp.cos(ang_r), zr), lanes(zc, jnp.cos(ang_c)),
            lanes(jnp.sin(ang_r), zr), lanes(zc, jnp.sin(ang_c)))


def _identity_rope_tables(rows):
    z_row = jnp.zeros((rows, ROPE_LANES), F32)
    z_col = jnp.zeros((GRID_W, ROPE_LANES), F32)
    return (z_row, jnp.ones((GRID_W, ROPE_LANES), F32), z_row, z_col)


def _bias_tables(rpb):
    qc = np.arange(GRID_W)[:, None]
    kc = np.arange(GRID_W)[None, :]
    cs = np.clip(qc - WIN_W // 2, 0, GRID_W - WIN_W)
    inband = (kc >= cs) & (kc < cs + WIN_W)
    dc = kc - qc + (WIN_W - 1)
    onehot = (inband[None] & (dc[None] == np.arange(2 * WIN_W - 1)[:, None, None])).astype(np.float32)
    m = jnp.einsum("hdc,cqk->hdqk", rpb * LOG2E, onehot, precision=lax.Precision.HIGHEST)
    m = jnp.where(inband, m, NEG_BIAS)
    n_dr = 2 * WIN_H - 1
    m = m.reshape(HEAD_GROUPS, GROUP_HEADS, n_dr, GRID_W, GRID_W).transpose(0, 2, 1, 3, 4)
    m = m.reshape(HEAD_GROUPS, n_dr, GROUP_HEADS * GRID_W, GRID_W)
    return jnp.concatenate([m[:, :-1], m[:, 1:]], axis=-1)


def kernel(x, c, ctx, c_ctx, w_ada, b_ada, norm_g, w_ff1_up, w_ff1_down, w_in, b_gate, rpb,
           ln_v_g, ln_v_b, w_s, b_s, w_pa, w_pb, w_o, w_ff2_up, w_ff2_down, final_g):
    batch, n, _ = x.shape
    ctx_len = ctx.shape[1]
    depth = w_ada.shape[0]
    assert n % (8 * GRID_W) == 0 and ctx_len % SG_CHUNK == 0 and batch + 1 <= COND_ROWS
    tm_x, tm_c = 512, ctx_len

    cond = jnp.zeros((COND_ROWS, D_MODEL), F32).at[:batch].set(c).at[batch].set(c_ctx)
    mods = _ada(cond, w_ada, b_ada).reshape(depth, COND_ROWS, N_MOD, D_MODEL)

    rope_x = _rope_tables(n // GRID_W)
    rope_c = _identity_rope_tables(ctx_len // GRID_W)

    ff1 = (w_ff1_up.astype(BF16), w_ff1_down.astype(BF16))
    ff2 = (w_ff2_up.astype(BF16), w_ff2_down.astype(BF16))
    w_in_b = w_in.astype(BF16)
    mix_w = (w_pa.astype(BF16), w_pb.astype(BF16), w_o.astype(BF16))
    w_s_b = w_s.astype(BF16)

    for l in range(depth):
        last = l == depth - 1
        mx = mods[l, :batch]
        mc = jnp.broadcast_to(mods[l, batch:batch + 1], (batch, N_MOD, D_MODEL))
        bias = _bias_tables(rpb[l])
        bs_full = jnp.repeat(b_s[l].T, SG_WIDTH // SG_GROUPS, axis=1)

        x = _ffn(x, mx, norm_g[l, 0], *ff1, l, tm=tm_x)
        ctx = _ffn(ctx, mc, norm_g[l, 0], *ff1, l, tm=tm_c)

        qx, kx, vx, ux, vnx, glx = _inproj(x, mx, norm_g[l, 1], w_in_b, l, rope_x,
                                           ln_v_g[l], ln_v_b[l], tm=tm_x)
        qc, kc, vc, uc, vnc, glc = _inproj(ctx, mc, norm_g[l, 1], w_in_b, l, rope_c,
                                           ln_v_g[l], ln_v_b[l], tm=tm_c)
        oa = _attn(qx, kx, vx, kc, vc, bias)
        x = _merge_ffn(x, oa, ux, vnx, glx, mx, w_s_b, bs_full, b_gate[l], *mix_w, norm_g[l, 2], *ff2, l,
                       final_g, final=last, tm=tm_x)
        if not last:
            oac = _ctx_attn(qc, kc, vc)
            ctx = _merge_ffn(ctx, oac, uc, vnc, glc, mc, w_s_b, bs_full, b_gate[l], *mix_w, norm_g[l, 2],
                             *ff2, l, final_g, final=False, tm=tm_c)

    return x
```

```python
import functools
import math

import jax
import jax.numpy as jnp
import numpy as np
from jax import lax
from jax.experimental import pallas as pl
from jax.experimental.pallas import tpu as pltpu

D_MODEL = 1024
GRID_W = 64
NA_HEADS = 8
HEAD_DIM = 64
NA_WIDTH = NA_HEADS * HEAD_DIM
WIN_H = 8
WIN_W = 16
SG_GROUPS = 4
SG_CHUNK = 128
SG_WIDTH = 512
D_FF = 2816
ROPE_THETA = 10000.0
EPS = 1e-6
N_MOD = 9
G0 = 3 * NA_WIDTH + 2 * SG_WIDTH
IN_COLS = G0 + 2 * D_MODEL

HEAD_GROUPS = 2
GROUP_HEADS = NA_HEADS // HEAD_GROUPS
GROUP_LANES = GROUP_HEADS * HEAD_DIM
ROPE_LANES = GROUP_LANES // 2
FF_CHUNK = 256
N_FF_CHUNKS = D_FF // FF_CHUNK
ADA_COLS = 2304
LOG2E = math.log2(math.e)
Q_SCALE = HEAD_DIM ** -0.5 * LOG2E
COND_ROWS = 8
NEG_BIAS = -1e30
VMEM_LIMIT = 56 * 1024 * 1024

BF16 = jnp.bfloat16
F32 = jnp.float32


def _dot(a, b):
    return jnp.dot(a, b, preferred_element_type=F32)


def _dot_nt(a, b):
    return lax.dot_general(a, b, (((1,), (1,)), ((), ())), preferred_element_type=F32)


def _rms_modulate(x, g, shift, scale):
    ms = jnp.mean(x * x, axis=-1, keepdims=True)
    return x * lax.rsqrt(ms + EPS) * (g * (1.0 + scale)) + shift


def _const_spec(shape):
    zeros = (0,) * len(shape)
    return pl.BlockSpec(shape, lambda *_: zeros, pipeline_mode=pl.Buffered(1))


def _layer_spec(shape, layer):
    index = (layer,) + (0,) * len(shape)
    return pl.BlockSpec((None,) + tuple(shape), lambda *_: index, pipeline_mode=pl.Buffered(1))


def _ada_kernel(cond_ref, w_ref, b_ref, out_ref):
    c = cond_ref[...]
    s = c * jax.nn.sigmoid(c)
    w = w_ref[0]
    s_hi = s.astype(BF16)
    s_lo = (s - s_hi.astype(F32)).astype(BF16)
    w_hi = w.astype(BF16)
    w_lo = (w - w_hi.astype(F32)).astype(BF16)
    acc = _dot(s_hi, w_hi) + _dot(s_hi, w_lo) + _dot(s_lo, w_hi)
    out_ref[0] = acc + b_ref[0]


def _ada(cond, w_ada, b_ada):
    depth = w_ada.shape[0]
    cols = w_ada.shape[2]
    return pl.pallas_call(
        _ada_kernel,
        grid=(depth, cols // ADA_COLS),
        in_specs=[
            pl.BlockSpec((COND_ROWS, D_MODEL), lambda l, j: (0, 0)),
            pl.BlockSpec((1, D_MODEL, ADA_COLS), lambda l, j: (l, 0, j)),
            pl.BlockSpec((1, 1, ADA_COLS), lambda l, j: (l, 0, j)),
        ],
        out_specs=pl.BlockSpec((1, COND_ROWS, ADA_COLS), lambda l, j: (l, 0, j)),
        out_shape=jax.ShapeDtypeStruct((depth, COND_ROWS, cols), F32),
        compiler_params=pltpu.CompilerParams(
            dimension_semantics=("parallel", "parallel"), vmem_limit_bytes=VMEM_LIMIT),
        name="ada",
    )(cond, w_ada, b_ada.reshape(depth, 1, cols))


def _ffn_body(x_ref, mod_ref, g_ref, wup_ref, wd_ref, fg_ref, out_ref, h_scr, acc_scr, *, mod_row, final):
    x = x_ref[...]
    shift = mod_ref[mod_row:mod_row + 1, :]
    scale = mod_ref[mod_row + 1:mod_row + 2, :]
    gate = mod_ref[mod_row + 2:mod_row + 3, :]
    h_scr[...] = _rms_modulate(x, g_ref[...], shift, scale).astype(BF16)

    for c in range(N_FF_CHUNKS):
        cols = slice(c * FF_CHUNK, (c + 1) * FF_CHUNK)
        gate_cols = slice(D_FF + c * FF_CHUNK, D_FF + (c + 1) * FF_CHUNK)
        h = h_scr[...]
        a = _dot(h, wup_ref[:, cols])
        b = _dot(h, wup_ref[:, gate_cols])
        act = (a * jax.nn.sigmoid(a) * b).astype(BF16)
        y = _dot(act, wd_ref[cols, :])
        if c == 0:
            acc_scr[...] = y
        else:
            acc_scr[...] += y
    o = x_ref[...] + (0.5 * gate) * acc_scr[...]
    if final:
        ms = jnp.mean(o * o, axis=-1, keepdims=True)
        o = o * lax.rsqrt(ms + EPS) * fg_ref[...]
    out_ref[...] = o


def _ffn_kernel(x_ref, mod_ref, g_ref, wup_ref, wd_ref, out_ref, h_scr, acc_scr):
    _ffn_body(x_ref, mod_ref, g_ref, wup_ref, wd_ref, None, out_ref, h_scr, acc_scr, mod_row=0, final=False)


def _ffn(x, mod, g, w_up, w_down, layer, *, tm):
    batch, n, _ = x.shape
    return pl.pallas_call(
        _ffn_kernel,
        grid=(batch, n // tm),
        in_specs=[
            pl.BlockSpec((None, tm, D_MODEL), lambda b, i: (b, i, 0)),
            pl.BlockSpec((None, N_MOD, D_MODEL), lambda b, i: (b, 0, 0)),
            _const_spec((1, D_MODEL)),
            _layer_spec((D_MODEL, 2 * D_FF), layer),
            _layer_spec((D_FF, D_MODEL), layer),
        ],
        out_specs=pl.BlockSpec((None, tm, D_MODEL), lambda b, i: (b, i, 0)),
        out_shape=jax.ShapeDtypeStruct(x.shape, F32),
        scratch_shapes=[pltpu.VMEM((tm, D_MODEL), BF16), pltpu.VMEM((tm, D_MODEL), F32)],
        compiler_params=pltpu.CompilerParams(
            dimension_semantics=("parallel", "parallel"), vmem_limit_bytes=VMEM_LIMIT),
        name="ffn",
    )(x, mod, g.reshape(1, D_MODEL), w_up, w_down)


def _deinterleave_matrix():
    old = lax.broadcasted_iota(jnp.int32, (GROUP_LANES, GROUP_LANES), 0)
    new = lax.broadcasted_iota(jnp.int32, (GROUP_LANES, GROUP_LANES), 1)
    half = new // ROPE_LANES
    head = (new % ROPE_LANES) // (HEAD_DIM // 2)
    pair = new % (HEAD_DIM // 2)
    return jnp.where(old == HEAD_DIM * head + 2 * pair + half, 1.0, 0.0).astype(BF16)


def _inproj_kernel(x_ref, mod_ref, g_ref, w_ref, rcos_ref, ccos_ref, rsin_ref, csin_ref, lng_ref, lnb_ref,
                   q_ref, k_ref, v_ref, u_ref, vn_ref, gl_ref, wqk_scr, h_scr):
    @pl.when((pl.program_id(0) == 0) & (pl.program_id(1) == 0))
    def _():
        perm = _deinterleave_matrix()
        for t in range(2 * HEAD_GROUPS):
            cols = slice(t * GROUP_LANES, (t + 1) * GROUP_LANES)
            wqk_scr[:, cols] = _dot(w_ref[:, cols], perm).astype(BF16)

    h_scr[...] = _rms_modulate(x_ref[...], g_ref[...], mod_ref[3:4, :], mod_ref[4:5, :]).astype(BF16)

    def proj(w_cols):
        return _dot(h_scr[...], w_cols)

    def token_table(by_row_ref, by_col_ref):
        by_col = by_col_ref[...]
        return jnp.concatenate([by_row_ref[r:r + 1, :] + by_col for r in range(by_row_ref.shape[0])], axis=0)

    cos = token_table(rcos_ref, ccos_ref)
    sin = token_table(rsin_ref, csin_ref)

    def rope_store(t, out_ref, mult):
        for g in range(HEAD_GROUPS):
            lo = g * GROUP_LANES
            e = t[:, lo:lo + ROPE_LANES]
            o = t[:, lo + ROPE_LANES:lo + GROUP_LANES]
            out_ref[:, lo:lo + ROPE_LANES] = ((e * cos - o * sin) * mult).astype(BF16)
            out_ref[:, lo + ROPE_LANES:lo + GROUP_LANES] = ((e * sin + o * cos) * mult).astype(BF16)

    w = NA_WIDTH
    uv = jax.nn.gelu(proj(w_ref[:, 3 * w:G0]))
    u_ref[...] = uv[:, :SG_WIDTH].astype(BF16)
    vs = uv[:, SG_WIDTH:]
    mu = jnp.mean(vs, axis=-1, keepdims=True)
    dv = vs - mu
    var = jnp.mean(dv * dv, axis=-1, keepdims=True)
    vn_ref[...] = (dv * lax.rsqrt(var + EPS) * lng_ref[...] + lnb_ref[...]).astype(BF16)
    gl_ref[...] = proj(w_ref[:, G0:IN_COLS])
    rope_store(proj(wqk_scr[:, 0:w]), q_ref, Q_SCALE)
    rope_store(proj(wqk_scr[:, w:2 * w]), k_ref, 1.0)
    v_ref[...] = proj(w_ref[:, 2 * w:3 * w]).astype(BF16)


def _inproj(x, mod, g, w_in, layer, rope, ln_g, ln_b, *, tm):
    batch, n, _ = x.shape
    tile_rows = tm // GRID_W
    by_row_spec = pl.BlockSpec((tile_rows, ROPE_LANES), lambda b, i: (i, 0))
    by_col_spec = _const_spec((GRID_W, ROPE_LANES))
    tok = lambda width, dt: jax.ShapeDtypeStruct((batch, n, width), dt)
    tok_spec = lambda width: pl.BlockSpec((None, tm, width), lambda b, i: (b, i, 0))
    return pl.pallas_call(
        _inproj_kernel,
        grid=(batch, n // tm),
        in_specs=[
            tok_spec(D_MODEL),
            pl.BlockSpec((None, N_MOD, D_MODEL), lambda b, i: (b, 0, 0)),
            _const_spec((1, D_MODEL)),
            _layer_spec((D_MODEL, IN_COLS), layer),
            by_row_spec, by_col_spec, by_row_spec, by_col_spec,
            _const_spec((1, SG_WIDTH)),
            _const_spec((1, SG_WIDTH)),
        ],
        out_specs=[tok_spec(NA_WIDTH), tok_spec(NA_WIDTH), tok_spec(NA_WIDTH),
                   tok_spec(SG_WIDTH), tok_spec(SG_WIDTH), tok_spec(2 * D_MODEL)],
        out_shape=[tok(NA_WIDTH, BF16), tok(NA_WIDTH, BF16), tok(NA_WIDTH, BF16),
                   tok(SG_WIDTH, BF16), tok(SG_WIDTH, BF16), tok(2 * D_MODEL, F32)],
        scratch_shapes=[pltpu.VMEM((D_MODEL, 2 * NA_WIDTH), BF16), pltpu.VMEM((tm, D_MODEL), BF16)],
        compiler_params=pltpu.CompilerParams(
            dimension_semantics=("arbitrary", "arbitrary"), vmem_limit_bytes=VMEM_LIMIT),
        name="inproj",
    )(x, mod, g.reshape(1, D_MODEL), w_in, *rope,
      ln_g.reshape(1, SG_WIDTH), ln_b.reshape(1, SG_WIDTH))


def _head_row_mask():
    row_head = lax.broadcasted_iota(jnp.int32, (GROUP_LANES, GROUP_LANES), 0) // GRID_W
    lane_head = (lax.broadcasted_iota(jnp.int32, (GROUP_LANES, GROUP_LANES), 1) % ROPE_LANES) // (HEAD_DIM // 2)
    return row_head == lane_head


def _unstack_heads(o, rows):
    lane_head = lax.broadcasted_iota(jnp.int32, (rows, GROUP_LANES), 1) // HEAD_DIM
    out = o[(GROUP_HEADS - 1) * rows:, :]
    for j in range(GROUP_HEADS - 2, -1, -1):
        out = jnp.where(lane_head == j, o[j * rows:(j + 1) * rows, :], out)
    return out


def _attn_kernel(q_ref, k_ref, v_ref, kc_ref, vc_ref, bias_ref, o_ref, *, rows, block_rows, halo_rows):
    r0 = pl.program_id(1) * block_rows
    halo_start = jnp.clip(r0 - WIN_H // 2, 0, rows - halo_rows)
    qmask = _head_row_mask()

    def row_body(ri):
        r = r0 + ri
        rs = jnp.clip(r - WIN_H // 2, 0, rows - WIN_H)
        koff = pl.multiple_of((rs - halo_start) * GRID_W, GRID_W)
        d0 = rs - r + (WIN_H - 1)
        qoff = ri * GRID_W
        for g in range(HEAD_GROUPS):
            lanes = slice(g * GROUP_LANES, (g + 1) * GROUP_LANES)
            qg = q_ref[pl.ds(qoff, GRID_W), lanes]
            qs = jnp.where(qmask, jnp.concatenate([qg] * GROUP_HEADS, axis=0), jnp.zeros((), BF16))
            kw = k_ref[pl.ds(koff, WIN_H * GRID_W), lanes]
            bias = jnp.concatenate([bias_ref[g, d0 + 2 * p] for p in range(WIN_H // 2)], axis=-1)
            s_nb = _dot_nt(qs, kw) + bias
            s_c = _dot_nt(qs, kc_ref[:, lanes])
            m = jnp.maximum(jnp.max(s_nb, axis=-1, keepdims=True),
                            jnp.max(s_c, axis=-1, keepdims=True))
            p_nb = jnp.exp2(s_nb - m)
            p_c = jnp.exp2(s_c - m)
            denom = jnp.sum(p_nb, axis=-1, keepdims=True) + jnp.sum(p_c, axis=-1, keepdims=True)
            vw = v_ref[pl.ds(koff, WIN_H * GRID_W), lanes]
            o = _dot(p_nb.astype(BF16), vw) + _dot(p_c.astype(BF16), vc_ref[:, lanes])
            o = o / denom
            o_ref[pl.ds(qoff, GRID_W), lanes] = _unstack_heads(o, GRID_W).astype(BF16)

    for ri in range(block_rows):
        row_body(ri)


def _attn(q, k, v, kc, vc, bias):
    batch, n, _ = q.shape
    ctx_len = kc.shape[1]
    rows = n // GRID_W
    block_rows = 8
    halo_rows = block_rows + WIN_H
    tm = block_rows * GRID_W

    def halo_map(b, i):
        start = jnp.clip(i * block_rows - WIN_H // 2, 0, rows - halo_rows)
        return (b, start * GRID_W, 0)

    halo_spec = pl.BlockSpec((None, pl.Element(halo_rows * GRID_W), pl.Element(NA_WIDTH)), halo_map)
    kern = functools.partial(_attn_kernel, rows=rows, block_rows=block_rows, halo_rows=halo_rows)
    return pl.pallas_call(
        kern,
        grid=(batch, rows // block_rows),
        in_specs=[
            pl.BlockSpec((None, tm, NA_WIDTH), lambda b, i: (b, i, 0)),
            halo_spec,
            halo_spec,
            pl.BlockSpec((None, ctx_len, NA_WIDTH), lambda b, i: (b, 0, 0)),
            pl.BlockSpec((None, ctx_len, NA_WIDTH), lambda b, i: (b, 0, 0)),
            _const_spec(bias.shape),
        ],
        out_specs=pl.BlockSpec((None, tm, NA_WIDTH), lambda b, i: (b, i, 0)),
        out_shape=jax.ShapeDtypeStruct((batch, n, NA_WIDTH), BF16),
        compiler_params=pltpu.CompilerParams(
            dimension_semantics=("parallel", "parallel"), vmem_limit_bytes=VMEM_LIMIT),
        name="nbr_attn",
    )(q, k, v, kc, vc, bias)


def _ctx_attn_kernel(q_ref, k_ref, v_ref, o_ref):
    n = q_ref.shape[0]
    lane_head_qk = (lax.broadcasted_iota(jnp.int32, (n, GROUP_LANES), 1) % ROPE_LANES) // (HEAD_DIM // 2)
    lane_head_v = lax.broadcasted_iota(jnp.int32, (n, GROUP_LANES), 1) // HEAD_DIM
    for g in range(HEAD_GROUPS):
        lanes = slice(g * GROUP_LANES, (g + 1) * GROUP_LANES)
        qg = q_ref[:, lanes]
        kg = k_ref[:, lanes]
        vg = v_ref[:, lanes]
        out = jnp.zeros((n, GROUP_LANES), F32)
        for j in range(GROUP_HEADS):
            qj = jnp.where(lane_head_qk == j, qg, jnp.zeros((), BF16))
            s = _dot_nt(qj, kg)
            p = jnp.exp2(s - jnp.max(s, axis=-1, keepdims=True))
            o = _dot(p.astype(BF16), vg) / jnp.sum(p, axis=-1, keepdims=True)
            out = out + jnp.where(lane_head_v == j, o, 0.0)
        o_ref[:, lanes] = out.astype(BF16)


def _ctx_attn(q, k, v):
    batch, n, width = q.shape
    spec = pl.BlockSpec((None, n, width), lambda b: (b, 0, 0))
    return pl.pallas_call(
        _ctx_attn_kernel,
        grid=(batch,),
        in_specs=[spec, spec, spec],
        out_specs=spec,
        out_shape=jax.ShapeDtypeStruct(q.shape, BF16),
        compiler_params=pltpu.CompilerParams(dimension_semantics=("parallel",)),
        name="ctx_attn",
    )(q, k, v)


def _merge_ffn_kernel(x_ref, oa_ref, u_ref, vn_ref, gl_ref, mod_ref, ws_ref, bs_ref, bg_ref,
                      wpa_ref, wpb_ref, wo_ref, g_ref, wup_ref, wd_ref, fg_ref, out_ref,
                      ob_scr, h_scr, acc_scr, *, final):
    tm = x_ref.shape[0]
    gw = SG_WIDTH // SG_GROUPS
    for n in range(tm // SG_CHUNK):
        rows = slice(n * SG_CHUNK, (n + 1) * SG_CHUNK)
        for g in range(SG_GROUPS):
            lanes = slice(g * gw, (g + 1) * gw)
            s = _dot(ws_ref[g], vn_ref[rows, lanes]) + bs_ref[:, lanes]
            ob_scr[rows, lanes] = (u_ref[rows, lanes].astype(F32) * s).astype(BF16)
    a = _dot(oa_ref[...], wpa_ref[...])
    b = _dot(ob_scr[...], wpb_ref[...])
    gates = jax.nn.sigmoid(gl_ref[...] + bg_ref[...])
    mixed = (gates[:, :D_MODEL] * a + gates[:, D_MODEL:] * b).astype(BF16)
    out_ref[...] = x_ref[...] + mod_ref[5:6, :] * _dot(mixed, wo_ref[...])
    _ffn_body(out_ref, mod_ref, g_ref, wup_ref, wd_ref, fg_ref, out_ref, h_scr, acc_scr,
              mod_row=6, final=final)


def _merge_ffn(x, oa, u, vn, gl, mod, w_s, bs_full, b_gate, w_pa, w_pb, w_o, g, w_up, w_down, layer,
               final_g, *, final, tm):
    batch, n, _ = x.shape
    tok_spec = lambda width: pl.BlockSpec((None, tm, width), lambda b, i: (b, i, 0))
    return pl.pallas_call(
        functools.partial(_merge_ffn_kernel, final=final),
        grid=(batch, n // tm),
        in_specs=[
            tok_spec(D_MODEL), tok_spec(NA_WIDTH), tok_spec(SG_WIDTH), tok_spec(SG_WIDTH),
            tok_spec(2 * D_MODEL),
            pl.BlockSpec((None, N_MOD, D_MODEL), lambda b, i: (b, 0, 0)),
            _layer_spec(w_s.shape[1:], layer),
            _const_spec(bs_full.shape),
            _const_spec((1, 2 * D_MODEL)),
            _layer_spec(w_pa.shape[1:], layer),
            _layer_spec(w_pb.shape[1:], layer),
            _layer_spec(w_o.shape[1:], layer),
            _const_spec((1, D_MODEL)),
            _layer_spec((D_MODEL, 2 * D_FF), layer),
            _layer_spec((D_FF, D_MODEL), layer),
            _const_spec((1, D_MODEL)),
        ],
        out_specs=tok_spec(D_MODEL),
        out_shape=jax.ShapeDtypeStruct(x.shape, F32),
        scratch_shapes=[pltpu.VMEM((tm, SG_WIDTH), BF16), pltpu.VMEM((tm, D_MODEL), BF16),
                        pltpu.VMEM((tm, D_MODEL), F32)],
        compiler_params=pltpu.CompilerParams(
            dimension_semantics=("parallel", "parallel"), vmem_limit_bytes=VMEM_LIMIT),
        name="merge_ffn_final" if final else "merge_ffn",
    )(x, oa, u, vn, gl, mod, w_s, bs_full, b_gate.reshape(1, 2 * D_MODEL), w_pa, w_pb, w_o,
      g.reshape(1, D_MODEL), w_up, w_down, final_g.reshape(1, D_MODEL))


def _rope_tables(rows):
    n_freq = HEAD_DIM // 4
    freqs = ROPE_THETA ** (-jnp.arange(n_freq, dtype=F32) / n_freq)
    ang_r = jnp.arange(rows, dtype=F32)[:, None] * freqs
    ang_c = jnp.arange(GRID_W, dtype=F32)[:, None] * freqs

    def lanes(row_part, col_part):
        return jnp.tile(jnp.concatenate([row_part, col_part], axis=-1), (1, GROUP_HEADS))

    zr, zc = jnp.zeros_like(ang_r), jnp.zeros_like(ang_c)
    return (lanes(jnp.cos(ang_r), zr), lanes(zc, jnp.cos(ang_c)),
            lanes(jnp.sin(ang_r), zr), lanes(zc, jnp.sin(ang_c)))


def _identity_rope_tables(rows):
    z_row = jnp.zeros((rows, ROPE_LANES), F32)
    z_col = jnp.zeros((GRID_W, ROPE_LANES), F32)
    return (z_row, jnp.ones((GRID_W, ROPE_LANES), F32), z_row, z_col)


def _bias_tables(rpb):
    qc = np.arange(GRID_W)[:, None]
    kc = np.arange(GRID_W)[None, :]
    cs = np.clip(qc - WIN_W // 2, 0, GRID_W - WIN_W)
    inband = (kc >= cs) & (kc < cs + WIN_W)
    dc = kc - qc + (WIN_W - 1)
    onehot = (inband[None] & (dc[None] == np.arange(2 * WIN_W - 1)[:, None, None])).astype(np.float32)
    m = jnp.einsum("hdc,cqk->hdqk", rpb * LOG2E, onehot, precision=lax.Precision.HIGHEST)
    m = jnp.where(inband, m, NEG_BIAS)
    n_dr = 2 * WIN_H - 1
    m = m.reshape(HEAD_GROUPS, GROUP_HEADS, n_dr, GRID_W, GRID_W).transpose(0, 2, 1, 3, 4)
    m = m.reshape(HEAD_GROUPS, n_dr, GROUP_HEADS * GRID_W, GRID_W)
    return jnp.concatenate([m[:, :-1], m[:, 1:]], axis=-1)


def kernel(x, c, ctx, c_ctx, w_ada, b_ada, norm_g, w_ff1_up, w_ff1_down, w_in, b_gate, rpb,
           ln_v_g, ln_v_b, w_s, b_s, w_pa, w_pb, w_o, w_ff2_up, w_ff2_down, final_g):
    batch, n, _ = x.shape
    ctx_len = ctx.shape[1]
    depth = w_ada.shape[0]
    assert n % (8 * GRID_W) == 0 and ctx_len % SG_CHUNK == 0 and batch + 1 <= COND_ROWS
    tm_x, tm_c = 512, ctx_len

    cond = jnp.zeros((COND_ROWS, D_MODEL), F32).at[:batch].set(c).at[batch].set(c_ctx)
    mods = _ada(cond, w_ada, b_ada).reshape(depth, COND_ROWS, N_MOD, D_MODEL)

    rope_x = _rope_tables(n // GRID_W)
    rope_c = _identity_rope_tables(ctx_len // GRID_W)

    ff1 = (w_ff1_up.astype(BF16), w_ff1_down.astype(BF16))
    ff2 = (w_ff2_up.astype(BF16), w_ff2_down.astype(BF16))
    w_in_b = w_in.astype(BF16)
    mix_w = (w_pa.astype(BF16), w_pb.astype(BF16), w_o.astype(BF16))
    w_s_b = w_s.astype(BF16)

    for l in range(depth):
        last = l == depth - 1
        mx = mods[l, :batch]
        mc = jnp.broadcast_to(mods[l, batch:batch + 1], (batch, N_MOD, D_MODEL))
        bias = _bias_tables(rpb[l])
        bs_full = jnp.repeat(b_s[l].T, SG_WIDTH // SG_GROUPS, axis=1)

        x = _ffn(x, mx, norm_g[l, 0], *ff1, l, tm=2 * tm_x)
        ctx = _ffn(ctx, mc, norm_g[l, 0], *ff1, l, tm=tm_c)

        qx, kx, vx, ux, vnx, glx = _inproj(x, mx, norm_g[l, 1], w_in_b, l, rope_x,
                                           ln_v_g[l], ln_v_b[l], tm=2 * tm_x)
        qc, kc, vc, uc, vnc, glc = _inproj(ctx, mc, norm_g[l, 1], w_in_b, l, rope_c,
                                           ln_v_g[l], ln_v_b[l], tm=tm_c)
        oa = _attn(qx, kx, vx, kc, vc, bias)
        x = _merge_ffn(x, oa, ux, vnx, glx, mx, w_s_b, bs_full, b_gate[l], *mix_w, norm_g[l, 2], *ff2, l,
                       final_g, final=last, tm=tm_x)
        if not last:
            oac = _ctx_attn(qc, kc, vc)
            ctx = _merge_ffn(ctx, oac, uc, vnc, glc, mc, w_s_b, bs_full, b_gate[l], *mix_w, norm_g[l, 2],
                             *ff2, l, final_g, final=False, tm=tm_c)

    return x
```
